```python
import jax, jax.numpy as jnp
from jax import lax
import numpy as np

D_MODEL = 2048
BATCH = 2
SEQ = 8192
DEPTH = 4

CHUNK = 64
MIX_A = D_MODEL
MIX_B = D_MODEL
MIX = MIX_A + MIX_B
H_A = 8
DV_A = MIX_A // H_A
DK_A = DV_A // 2
QK_A = H_A * DK_A
H_B = 8
DB = MIX_B // H_B
CONV_W = 4
LRU_C = 8.0
EPS = 1e-6
SEG_SIZES = [2 * QK_A, MIX_A, MIX_A, MIX_A, H_A, H_A, MIX_B, MIX_B]
SPLIT_AT = [int(s) for s in np.cumsum(SEG_SIZES)[:-1]]
N_IN = int(sum(SEG_SIZES))

kernel_name = "hymba_style_mlstm_rglru_trunk"


def rms_norm(x, g):
    x32 = x.astype(jnp.float32)
    y = x32 * lax.rsqrt(jnp.mean(x32 * x32, axis=-1, keepdims=True) + EPS)
    return (y * g.astype(jnp.float32)).astype(x.dtype)


def causal_dwconv(x, w):
    c = x.shape[-1]
    return lax.conv_general_dilated(
        x, w[:, None, :].astype(x.dtype), window_strides=(1,), padding=[(CONV_W - 1, 0)],
        dimension_numbers=("NWC", "WIO", "NWC"), feature_group_count=c)


def mlstm_chunkwise(q, k, v, i_pre, log_f):
    b_, s_, h_, dk = q.shape
    dv = v.shape[-1]
    nc = s_ // CHUNK

    def chunks(t):
        return t.reshape(b_, nc, CHUNK, h_, t.shape[-1]).transpose(1, 0, 3, 2, 4)

    def gchunks(t):
        return t.reshape(b_, nc, CHUNK, h_).transpose(1, 0, 3, 2)

    causal = jnp.tril(jnp.ones((CHUNK, CHUNK), dtype=bool))

    def step(carry, xs):
        c_st, n_st, m_st = carry
        qc, kc, vc, ic, fc = xs
        bcum = jnp.cumsum(fc, axis=-1)
        dmat = jnp.where(causal, bcum[..., :, None] - bcum[..., None, :] + ic[..., None, :], -jnp.inf)
        inter = bcum + m_st[..., None]
        m_t = jnp.maximum(inter, jnp.max(dmat, axis=-1))
        w_inter = jnp.exp(inter - m_t)
        sc = jnp.einsum("bhtk,bhsk->bhts", qc, kc) * jnp.exp(dmat - m_t[..., None])
        num = jnp.einsum("bhts,bhsv->bhtv", sc, vc) + w_inter[..., None] * jnp.einsum("bhtk,bhkv->bhtv", qc, c_st)
        den = jnp.sum(sc, axis=-1) + w_inter * jnp.einsum("bhtk,bhk->bht", qc, n_st)
        hc = num / jnp.maximum(jnp.abs(den), jnp.exp(-m_t))[..., None]
        b_last = bcum[..., -1]
        g = b_last[..., None] - bcum + ic
        m_new = jnp.maximum(b_last + m_st, jnp.max(g, axis=-1))
        decay = jnp.exp(b_last + m_st - m_new)
        wk = jnp.exp(g - m_new[..., None])
        c_new = decay[..., None, None] * c_st + jnp.einsum("bhs,bhsk,bhsv->bhkv", wk, kc, vc)
        n_new = decay[..., None] * n_st + jnp.einsum("bhs,bhsk->bhk", wk, kc)
        return (c_new, n_new, m_new), hc

    init = (jnp.zeros((b_, h_, dk, dv), jnp.float32),
            jnp.zeros((b_, h_, dk), jnp.float32),
            jnp.zeros((b_, h_), jnp.float32))
    _, hs = lax.scan(step, init, (chunks(q), chunks(k), chunks(v), gchunks(i_pre), gchunks(log_f)))
    return hs.transpose(1, 0, 3, 2, 4).reshape(b_, s_, h_, dv)


def rg_lru(x, w_a, b_a, w_x, b_x, lam):
    b_, s_, _ = x.shape
    xh = x.reshape(b_, s_, H_B, DB)
    r = jax.nn.sigmoid(jnp.einsum("bshi,hij->bshj", xh, w_a.astype(jnp.float32)).reshape(b_, s_, MIX_B)
                       + b_a.astype(jnp.float32))
    i = jax.nn.sigmoid(jnp.einsum("bshi,hij->bshj", xh, w_x.astype(jnp.float32)).reshape(b_, s_, MIX_B)
                       + b_x.astype(jnp.float32))
    log_a = -LRU_C * r * jax.nn.softplus(-lam.astype(jnp.float32))
    a = jnp.exp(log_a)
    u = x * i * jnp.sqrt(-jnp.expm1(2.0 * log_a))

    def combine(left, right):
        a1, h1 = left
        a2, h2 = right
        return a1 * a2, a2 * h1 + h2

    _, h = lax.associative_scan(combine, (a, u), axis=1)
    return h


def hybrid_layer(x, norm_g, w_in, i_bias, f_bias, qk_conv, head_norm_g,
                 lru_conv_w, lru_conv_b, w_a, b_a, w_x, b_x, lam, w_out):
    b_, s_, _ = x.shape
    f32 = jnp.float32
    u = rms_norm(x, norm_g)
    p = u @ w_in
    qk, v, o, z_a, ig, fg, xb, z_b = jnp.split(p, SPLIT_AT, axis=-1)

    qk = jax.nn.silu(causal_dwconv(qk, qk_conv))
    q, k = jnp.split(qk, 2, axis=-1)
    q = q.reshape(b_, s_, H_A, DK_A).astype(f32)
    k = k.reshape(b_, s_, H_A, DK_A).astype(f32) * (DK_A ** -0.5)
    v = v.reshape(b_, s_, H_A, DV_A).astype(f32)
    i_pre = ig.astype(f32) + i_bias.astype(f32)
    log_f = jax.nn.log_sigmoid(fg.astype(f32) + f_bias.astype(f32))
    h = mlstm_chunkwise(q, k, v, i_pre, log_f)
    h = h * jax.nn.sigmoid(o.astype(f32)).reshape(b_, s_, H_A, DV_A)
    mu = jnp.mean(h, axis=-1, keepdims=True)
    var = jnp.mean(jnp.square(h - mu), axis=-1, keepdims=True)
    h = ((h - mu) * lax.rsqrt(var + EPS)).reshape(b_, s_, MIX_A) * head_norm_g.astype(f32)
    y_a = h.astype(x.dtype) * jax.nn.silu(z_a)

    xc = causal_dwconv(xb, lru_conv_w) + lru_conv_b
    y_b = rg_lru(xc.astype(f32), w_a, b_a, w_x, b_x, lam).astype(x.dtype) * jax.nn.silu(z_b)

    return x + jnp.concatenate([y_a, y_b], axis=-1) @ w_out


def setup_inputs(seed: int = 0) -> dict:
    key = jax.random.key(seed)
    ks = jax.random.split(key, 20)
    f32 = jnp.float32
    n = lambda k, shp: jax.random.normal(k, shp, f32)
    u_lam = jax.random.uniform(ks[13], (DEPTH, MIX_B), f32, 0.9, 0.999)
    s_lam = u_lam ** (1.0 / LRU_C)
    return {
        "x": n(ks[0], (BATCH, SEQ, D_MODEL)),
        "norm_g": 1.0 + 0.02 * n(ks[1], (DEPTH, D_MODEL)),
        "w_in": n(ks[2], (DEPTH, D_MODEL, N_IN)) * (D_MODEL ** -0.5),
        "i_bias": 0.1 * n(ks[3], (DEPTH, H_A)),
        "f_bias": jnp.linspace(3.0, 6.0, H_A, dtype=f32)[None, :] + 0.1 * n(ks[4], (DEPTH, H_A)),
        "qk_conv": n(ks[5], (DEPTH, CONV_W, 2 * QK_A)) * (CONV_W ** -0.5),
        "head_norm_g": 1.0 + 0.02 * n(ks[6], (DEPTH, MIX_A)),
        "lru_conv_w": n(ks[7], (DEPTH, CONV_W, MIX_B)) * (CONV_W ** -0.5),
        "lru_conv_b": 0.02 * n(ks[8], (DEPTH, MIX_B)),
        "w_a": n(ks[9], (DEPTH, H_B, DB, DB)) * (DB ** -0.5),
        "b_a": 0.02 * n(ks[10], (DEPTH, MIX_B)),
        "w_x": n(ks[11], (DEPTH, H_B, DB, DB)) * (DB ** -0.5),
        "b_x": 0.02 * n(ks[12], (DEPTH, MIX_B)),
        "lam": jnp.log(s_lam) - jnp.log1p(-s_lam),
        "w_out": n(ks[14], (DEPTH, MIX, D_MODEL)) * (MIX ** -0.5),
        "final_g": 1.0 + 0.02 * n(ks[15], (D_MODEL,)),
    }


def reference(x, norm_g, w_in, i_bias, f_bias, qk_conv, head_norm_g,
              lru_conv_w, lru_conv_b, w_a, b_a, w_x, b_x, lam, w_out, final_g):
    for l in range(DEPTH):
        x = hybrid_layer(x, norm_g[l], w_in[l], i_bias[l], f_bias[l], qk_conv[l], head_norm_g[l],
                         lru_conv_w[l], lru_conv_b[l], w_a[l], b_a[l], w_x[l], b_x[l], lam[l], w_out[l])
    return rms_norm(x, final_g)
```

```python
import functools

import jax
import jax.numpy as jnp
from jax import lax
from jax.experimental import pallas as pl
from jax.experimental.pallas import tpu as pltpu

f32 = jnp.float32
bf16 = jnp.bfloat16

D_MODEL = 2048
DEPTH = 4
H_A = 8
DK_A = 128
DV_A = 256
H_B = 8
DB = 256
CONV_W = 4
LRU_C = 8.0
EPS = 1e-6

LANES = 128
SUBLANES = 8
N_MAIN = 6 * D_MODEL
N_GROUPS = N_MAIN // LANES
G_Q, G_K, G_V, G_O, G_ZA, G_XB, G_ZB = 0, 8, 16, 32, 48, 64, 80
N_GROUPS_A = 64

TM_IN = 1024
TN_IN = 1024
LC = 256
T_LRU = 1024
TM_OUT = 512
TN_OUT = 1024
TM_NORM = 1024

NEG_BIG = -1e30
VMEM_LIMIT = 56 * 1024 * 1024


def _sigmoid(x):
    return 1.0 / (1.0 + jnp.exp(-x))


def _params(n_axes):
    return pltpu.CompilerParams(dimension_semantics=("arbitrary",) * n_axes, vmem_limit_bytes=VMEM_LIMIT)


def _inproj_kernel(x_ref, g_ref, w_ref, wg_ref, p_ref, gi_ref, gf_ref, u_ref):
    @pl.when(pl.program_id(1) == 0)
    def _():
        x = x_ref[...]
        ms = jnp.mean(x * x, axis=-1, keepdims=True)
        u = (x * lax.rsqrt(ms + EPS)) * g_ref[...]
        ub = u.astype(bf16)
        u_ref[...] = ub
        gg = jnp.dot(ub, wg_ref[...], preferred_element_type=f32)
        gi_ref[...] = gg[:, :LANES]
        gf_ref[...] = gg[:, LANES:]

    acc = jnp.dot(u_ref[...], w_ref[...], preferred_element_type=f32)
    for g in range(TN_IN // LANES):
        p_ref[g] = acc[:, g * LANES:(g + 1) * LANES].astype(bf16)


def _inproj(x2d, norm_g, w_main, w_gate):
    m = x2d.shape[0]
    grid = (m // TM_IN, N_MAIN // TN_IN)
    return pl.pallas_call(
        _inproj_kernel,
        grid=grid,
        in_specs=[
            pl.BlockSpec((TM_IN, D_MODEL), lambda i, j: (i, 0)),
            pl.BlockSpec((1, D_MODEL), lambda i, j: (0, 0)),
            pl.BlockSpec((D_MODEL, TN_IN), lambda i, j: (0, j)),
            pl.BlockSpec((D_MODEL, 2 * LANES), lambda i, j: (0, 0)),
        ],
        out_specs=[
            pl.BlockSpec((TN_IN // LANES, TM_IN, LANES), lambda i, j: (j, i, 0)),
            pl.BlockSpec((TM_IN, LANES), lambda i, j: (i, 0)),
            pl.BlockSpec((TM_IN, LANES), lambda i, j: (i, 0)),
        ],
        out_shape=[
            jax.ShapeDtypeStruct((N_GROUPS, m, LANES), bf16),
            jax.ShapeDtypeStruct((m, LANES), f32),
            jax.ShapeDtypeStruct((m, LANES), f32),
        ],
        scratch_shapes=[pltpu.VMEM((TM_IN, D_MODEL), bf16)],
        compiler_params=_params(2),
        name="inproj",
    )(x2d, norm_g, w_main, w_gate)


def _gate_kernel(gi_ref, gf_ref, ib_ref, fb_ref, a_ref, wi_ref, e_ref, wk_ref, rt_ref, dec_ref, m_ref):
    @pl.when(pl.program_id(1) == 0)
    def _():
        m_ref[...] = jnp.zeros_like(m_ref)

    ii = gi_ref[...] + ib_ref[...]
    lf = jax.nn.log_sigmoid(gf_ref[...] + fb_ref[...])
    row = lax.broadcasted_iota(jnp.int32, (LC, LANES), 0)

    bc = lf
    d = 1
    while d < LC:
        bc = bc + jnp.where(row >= d, pltpu.roll(bc, d, axis=0), 0.0)
        d *= 2
    r = ii - bc
    cm = r
    d = 1
    while d < LC:
        cm = jnp.maximum(cm, jnp.where(row >= d, pltpu.roll(cm, d, axis=0), NEG_BIG))
        d *= 2

    m_st = m_ref[0:1, :]
    mx = jnp.maximum(m_st, cm)
    a_ref[...] = -mx
    wi_ref[...] = jnp.exp(m_st - mx)
    e_ref[...] = jnp.exp(-(bc + mx))
    mx_last = mx[LC - 1:LC, :]
    wk_ref[...] = jnp.exp(r - mx_last)
    dec_ref[0] = jnp.broadcast_to(jnp.exp(m_st - mx_last), (SUBLANES, LANES))
    rt_ref[...] = r.T[0:SUBLANES, :]
    m_ref[...] = jnp.broadcast_to(bc[LC - 1:LC, :] + mx_last, (SUBLANES, LANES))


def _gate_prep(gi, gf, ib, fb, batch, nc):
    m = gi.shape[0]
    blk = pl.BlockSpec((LC, LANES), lambda b, c: (b * nc + c, 0))
    vec = pl.BlockSpec((1, LANES), lambda b, c: (0, 0))
    return pl.pallas_call(
        _gate_kernel,
        grid=(batch, nc),
        in_specs=[blk, blk, vec, vec],
        out_specs=[blk, blk, blk, blk,
                   pl.BlockSpec((SUBLANES, LC), lambda b, c: (0, b * nc + c)),
                   pl.BlockSpec((1, SUBLANES, LANES), lambda b, c: (b * nc + c, 0, 0))],
        out_shape=[jax.ShapeDtypeStruct((m, LANES), f32)] * 4
        + [jax.ShapeDtypeStruct((SUBLANES, m), f32),
           jax.ShapeDtypeStruct((batch * nc, SUBLANES, LANES), f32)],
        scratch_shapes=[pltpu.VMEM((SUBLANES, LANES), f32)],
        compiler_params=_params(2),
        name="gate_prep",
    )(gi, gf, ib, fb)


def _causal_conv(x, halo, w, row8):
    acc = x * w[CONV_W - 1:CONV_W, :]
    for j in range(1, CONV_W):
        xs = pltpu.roll(x, j, axis=0)
        hs = pltpu.roll(halo, j, axis=0)
        head = jnp.where(row8 < j, hs, xs[0:SUBLANES])
        xs = jnp.concatenate([head, xs[SUBLANES:]], axis=0)
        acc = acc + xs * w[CONV_W - 1 - j:CONV_W - j, :]
    return acc


def _mlstm_kernel(p_ref, a_ref, wi_ref, e_ref, wk_ref, rt_ref, dec_ref, cw_ref, hng_ref,
                  ya_ref, c_ref, n_ref, halo_ref):
    @pl.when(pl.program_id(1) == 0)
    def _():
        c_ref[...] = jnp.zeros_like(c_ref)
        n_ref[...] = jnp.zeros_like(n_ref)
        halo_ref[...] = jnp.zeros_like(halo_ref)

    row8 = lax.broadcasted_iota(jnp.int32, (SUBLANES, LANES), 0)
    causal = (lax.broadcasted_iota(jnp.int32, (LC, LC), 0)
              >= lax.broadcasted_iota(jnp.int32, (LC, LC), 1))

    def conv_silu(g):
        x = p_ref[g].astype(f32)
        y = _causal_conv(x, halo_ref[g], cw_ref[:, g * LANES:(g + 1) * LANES], row8)
        halo_ref[g] = x[LC - SUBLANES:LC]
        return y * _sigmoid(y)

    def wide(g0):
        return jnp.concatenate([p_ref[g0], p_ref[g0 + 1]], axis=-1)

    for h in range(H_A):
        q = conv_silu(G_Q + h)
        k = conv_silu(G_K + h) * (DK_A ** -0.5)
        qb = q.astype(bf16)
        kb = k.astype(bf16)
        v = wide(G_V + 2 * h)

        a_col = a_ref[:, h:h + 1]
        wi = wi_ref[:, h:h + 1]
        e_col = e_ref[:, h:h + 1]
        wk = wk_ref[:, h:h + 1]
        r_row = rt_ref[h:h + 1, :]
        dec = dec_ref[0, 0:1, h:h + 1]

        dmat = jnp.exp(jnp.where(causal, a_col + r_row, NEG_BIG))
        s = lax.dot_general(qb, kb, (((1,), (1,)), ((), ())), preferred_element_type=f32) * dmat
        c_st = c_ref[h]
        n_st = n_ref[h]
        num = (jnp.dot(s.astype(bf16), v, preferred_element_type=f32)
               + wi * jnp.dot(qb, c_st.astype(bf16), preferred_element_type=f32))
        den = jnp.sum(s, axis=-1, keepdims=True) + wi * jnp.sum(q * n_st, axis=-1, keepdims=True)
        hc = num * (1.0 / jnp.maximum(jnp.abs(den), e_col))

        kw = k * wk
        c_ref[h] = dec * c_st + jnp.dot(kw.T.astype(bf16), v, preferred_element_type=f32)
        n_ref[h] = dec * n_st + jnp.sum(kw, axis=0, keepdims=True)

        hg = hc * _sigmoid(wide(G_O + 2 * h).astype(f32))
        mu = jnp.mean(hg, axis=-1, keepdims=True)
        dev = hg - mu
        var = jnp.mean(dev * dev, axis=-1, keepdims=True)
        hn = (dev * lax.rsqrt(var + EPS)) * hng_ref[:, h * DV_A:(h + 1) * DV_A]
        z = wide(G_ZA + 2 * h).astype(f32)
        ya_ref[h] = (hn * (z * _sigmoid(z))).astype(bf16)


def _mlstm(p, a, wi, e, wk, rt, dec, conv_w, hn_g, batch, nc):
    m = p.shape[1]
    blk = pl.BlockSpec((LC, LANES), lambda b, c: (b * nc + c, 0))
    return pl.pallas_call(
        _mlstm_kernel,
        grid=(batch, nc),
        in_specs=[
            pl.BlockSpec((N_GROUPS_A, LC, LANES), lambda b, c: (0, b * nc + c, 0)),
            blk, blk, blk, blk,
            pl.BlockSpec((SUBLANES, LC), lambda b, c: (0, b * nc + c)),
            pl.BlockSpec((1, SUBLANES, LANES), lambda b, c: (b * nc + c, 0, 0)),
            pl.BlockSpec((CONV_W, 2 * H_A * DK_A), lambda b, c: (0, 0)),
            pl.BlockSpec((1, H_A * DV_A), lambda b, c: (0, 0)),
        ],
        out_specs=pl.BlockSpec((H_A, LC, DV_A), lambda b, c: (0, b * nc + c, 0)),
        out_shape=jax.ShapeDtypeStruct((H_A, m, DV_A), bf16),
        scratch_shapes=[
            pltpu.VMEM((H_A, DK_A, DV_A), f32),
            pltpu.VMEM((H_A, 1, DK_A), f32),
            pltpu.VMEM((2 * H_A, SUBLANES, LANES), f32),
        ],
        compiler_params=_params(2),
        name="mlstm",
    )(p, a, wi, e, wk, rt, dec, conv_w, hn_g)


def _lru_kernel(xb_ref, zb_ref, cw_ref, cb_ref, wa_ref, wx_ref, ba_ref, bx_ref, lam_ref,
                yb_ref, halo_ref, carry_ref):
    @pl.when(pl.program_id(2) == 0)
    def _():
        halo_ref[...] = jnp.zeros_like(halo_ref)
        carry_ref[...] = jnp.zeros_like(carry_ref)

    t_rows = T_LRU
    row8 = lax.broadcasted_iota(jnp.int32, (SUBLANES, DB), 0)
    x = jnp.concatenate([xb_ref[0], xb_ref[1]], axis=-1).astype(f32)
    xc = _causal_conv(x, halo_ref[...], cw_ref[...], row8) + cb_ref[...]
    halo_ref[...] = x[t_rows - SUBLANES:t_rows]

    xcb = xc.astype(bf16)
    r = _sigmoid(jnp.dot(xcb, wa_ref[0], preferred_element_type=f32) + ba_ref[...])
    i = _sigmoid(jnp.dot(xcb, wx_ref[0], preferred_element_type=f32) + bx_ref[...])
    lam = lam_ref[...]
    softplus_neg = jnp.maximum(-lam, 0.0) + jnp.log1p(jnp.exp(-jnp.abs(lam)))
    log_a = (-LRU_C * r) * softplus_neg
    a = jnp.exp(log_a)
    hloc = (xc * i) * jnp.sqrt(jnp.tanh(-log_a) * (a * a + 1.0))

    rmod = lax.broadcasted_iota(jnp.int32, (t_rows, DB), 0) % SUBLANES
    for d in (1, 2, 4):
        keep = rmod >= d
        h_sh = jnp.where(keep, pltpu.roll(hloc, d, axis=0), 0.0)
        a_sh = jnp.where(keep, pltpu.roll(a, d, axis=0), 1.0)
        hloc = hloc + a * h_sh
        a = a * a_sh

    z = jnp.concatenate([zb_ref[0], zb_ref[1]], axis=-1).astype(f32)
    gate = z * _sigmoid(z)
    carry = jnp.broadcast_to(carry_ref[0:1, :], (SUBLANES, DB))
    for g in range(t_rows // SUBLANES):
        lo = g * SUBLANES
        hg = hloc[lo:lo + SUBLANES] + a[lo:lo + SUBLANES] * carry
        yb_ref[0, lo:lo + SUBLANES, :] = (hg * gate[lo:lo + SUBLANES]).astype(bf16)
        carry = jnp.broadcast_to(hg[SUBLANES - 1:SUBLANES, :], (SUBLANES, DB))
    carry_ref[...] = carry


def _lru(p, conv_w, conv_b, w_a, w_x, b_a, b_x, lam, batch, nt):
    m = p.shape[1]
    vec = pl.BlockSpec((1, DB), lambda b, h, t: (0, h))
    return pl.pallas_call(
        _lru_kernel,
        grid=(batch, H_B, nt),
        in_specs=[
            pl.BlockSpec((2, T_LRU, LANES), lambda b, h, t: (G_XB // 2 + h, b * nt + t, 0)),
            pl.BlockSpec((2, T_LRU, LANES), lambda b, h, t: (G_ZB // 2 + h, b * nt + t, 0)),
            pl.BlockSpec((CONV_W, DB), lambda b, h, t: (0, h)),
            vec,
            pl.BlockSpec((1, DB, DB), lambda b, h, t: (h, 0, 0)),
            pl.BlockSpec((1, DB, DB), lambda b, h, t: (h, 0, 0)),
            vec, vec, vec,
        ],
        out_specs=pl.BlockSpec((1, T_LRU, DB), lambda b, h, t: (h, b * nt + t, 0)),
        out_shape=jax.ShapeDtypeStruct((H_B, m, DB), bf16),
        scratch_shapes=[pltpu.VMEM((SUBLANES, DB), f32), pltpu.VMEM((SUBLANES, DB), f32)],
        compiler_params=_params(3),
        name="rglru",
    )(p, p, conv_w, conv_b, w_a, w_x, b_a, b_x, lam)


def _outproj_kernel(ya_ref, yb_ref, w_ref, x_ref, o_ref):
    y = jnp.concatenate([ya_ref[h] for h in range(H_A)] + [yb_ref[h] for h in range(H_B)], axis=-1)
    o_ref[...] = x_ref[...] + jnp.dot(y, w_ref[...], preferred_element_type=f32)


def _outproj(ya, yb, w_out, x2d):
    m = x2d.shape[0]
    return pl.pallas_call(
        _outproj_kernel,
        grid=(D_MODEL // TN_OUT, m // TM_OUT),
        in_specs=[
            pl.BlockSpec((H_A, TM_OUT, DV_A), lambda j, i: (0, i, 0)),
            pl.BlockSpec((H_B, TM_OUT, DB), lambda j, i: (0, i, 0)),
            pl.BlockSpec((2 * D_MODEL, TN_OUT), lambda j, i: (0, j)),
            pl.BlockSpec((TM_OUT, TN_OUT), lambda j, i: (i, j)),
        ],
        out_specs=pl.BlockSpec((TM_OUT, TN_OUT), lambda j, i: (i, j)),
        out_shape=jax.ShapeDtypeStruct((m, D_MODEL), f32),
        compiler_params=_params(2),
        name="outproj",
    )(ya, yb, w_out, x2d)


def _norm_kernel(x_ref, g_ref, o_ref):
    x = x_ref[...]
    ms = jnp.mean(x * x, axis=-1, keepdims=True)
    o_ref[...] = (x * lax.rsqrt(ms + EPS)) * g_ref[...]


def _final_norm(x2d, g):
    m = x2d.shape[0]
    return pl.pallas_call(
        _norm_kernel,
        grid=(m // TM_NORM,),
        in_specs=[pl.BlockSpec((TM_NORM, D_MODEL), lambda i: (i, 0)),
                  pl.BlockSpec((1, D_MODEL), lambda i: (0, 0))],
        out_specs=pl.BlockSpec((TM_NORM, D_MODEL), lambda i: (i, 0)),
        out_shape=jax.ShapeDtypeStruct((m, D_MODEL), f32),
        compiler_params=_params(1),
        name="final_norm",
    )(x2d, g)


def _pad_lanes(v):
    return jnp.pad(v.astype(f32), (0, LANES - v.shape[0]))[None, :]


def kernel(x, norm_g, w_in, i_bias, f_bias, qk_conv, head_norm_g, lru_conv_w, lru_conv_b,
           w_a, b_a, w_x, b_x, lam, w_out, final_g):
    batch, seq, d = x.shape
    m = batch * seq
    nc = seq // LC
    nt = seq // T_LRU
    n_gate0 = 4 * D_MODEL
    x2d = x.reshape(m, d)
    for l in range(DEPTH):
        w = w_in[l]
        w_main = jnp.concatenate([w[:, :n_gate0], w[:, n_gate0 + 2 * H_A:]], axis=1).astype(bf16)
        w_gate = jnp.concatenate(
            [jnp.pad(w[:, n_gate0:n_gate0 + H_A], ((0, 0), (0, LANES - H_A))),
             jnp.pad(w[:, n_gate0 + H_A:n_gate0 + 2 * H_A], ((0, 0), (0, LANES - H_A)))],
            axis=1).astype(bf16)
        p, gi, gf = _inproj(x2d, norm_g[l][None, :], w_main, w_gate)
        a, wi, e, wk, rt, dec = _gate_prep(gi, gf, _pad_lanes(i_bias[l]), _pad_lanes(f_bias[l]), batch, nc)
        ya = _mlstm(p, a, wi, e, wk, rt, dec, qk_conv[l], head_norm_g[l][None, :], batch, nc)
        yb = _lru(p, lru_conv_w[l], lru_conv_b[l][None, :], w_a[l].astype(bf16), w_x[l].astype(bf16),
                  b_a[l][None, :], b_x[l][None, :], lam[l][None, :], batch, nt)
        x2d = _outproj(ya, yb, w_out[l].astype(bf16), x2d)
    return _final_norm(x2d, final_g[None, :]).reshape(batch, seq, d)
```

```python
import numpy as np

import jax
import jax.numpy as jnp
from jax import lax
from jax.experimental import pallas as pl
from jax.experimental.pallas import tpu as pltpu

f32 = jnp.float32
bf16 = jnp.bfloat16

D_MODEL = 2048
DEPTH = 4
H_A = 8
DK_A = 128
DV_A = 256
H_B = 8
DB = 256
CONV_W = 4
LRU_C = 8.0
EPS = 1e-6

LANES = 128
SUBLANES = 8
N_MAIN = 6 * D_MODEL
N_GROUPS = N_MAIN // LANES
G_Q, G_K, G_V, G_O, G_ZA, G_XB, G_ZB = 0, 8, 16, 32, 48, 64, 80
N_GROUPS_A = 64

TM_IN = 1024
TN_IN = 1024
PB = 256
SEG = PB // SUBLANES
LC = PB
T_LRU = 1024
TM_OUT = 512
TN_OUT = 1024
TM_NORM = 1024
HALO = (CONV_W - 1) * SUBLANES

NEG_BIG = -1e30
VMEM_LIMIT = 56 * 1024 * 1024


def _segment_major_matrix():
    p = np.arange(PB)
    pm = np.zeros((PB, PB), np.float32)
    pm[p, (p % SUBLANES) * SEG + p // SUBLANES] = 1.0
    return pm


_PM = _segment_major_matrix()


def _sigmoid(x):
    return 0.5 * jnp.tanh(0.5 * x) + 0.5


def _silu(x):
    hx = 0.5 * x
    return hx * jnp.tanh(hx) + hx


def _params(n_axes):
    return pltpu.CompilerParams(dimension_semantics=("arbitrary",) * n_axes, vmem_limit_bytes=VMEM_LIMIT)


def _inproj_kernel(x_ref, g_ref, w_ref, wg_ref, p_ref, gi_ref, gf_ref, u_ref):
    @pl.when(pl.program_id(1) == 0)
    def _():
        x = x_ref[...]
        ms = jnp.mean(x * x, axis=-1, keepdims=True)
        u = (x * lax.rsqrt(ms + EPS)) * g_ref[...]
        ub = u.astype(bf16)
        u_ref[...] = ub
        gg = jnp.dot(ub, wg_ref[...], preferred_element_type=f32)
        gi_ref[...] = gg[:, :LANES]
        gf_ref[...] = gg[:, LANES:]

    acc = jnp.dot(u_ref[...], w_ref[...], preferred_element_type=f32)
    for g in range(TN_IN // LANES):
        p_ref[g] = acc[:, g * LANES:(g + 1) * LANES].astype(bf16)


def _inproj(l, x2d, norm_g, w_main, w_gate):
    m = x2d.shape[0]
    grid = (m // TM_IN, N_MAIN // TN_IN)
    return pl.pallas_call(
        _inproj_kernel,
        grid=grid,
        in_specs=[
            pl.BlockSpec((TM_IN, D_MODEL), lambda i, j: (i, 0)),
            pl.BlockSpec((None, 1, D_MODEL), lambda i, j: (l, 0, 0)),
            pl.BlockSpec((None, D_MODEL, TN_IN), lambda i, j: (l, 0, j)),
            pl.BlockSpec((None, D_MODEL, 2 * LANES), lambda i, j: (l, 0, 0)),
        ],
        out_specs=[
            pl.BlockSpec((TN_IN // LANES, TM_IN, LANES), lambda i, j: (j, i, 0)),
            pl.BlockSpec((TM_IN, LANES), lambda i, j: (i, 0)),
            pl.BlockSpec((TM_IN, LANES), lambda i, j: (i, 0)),
        ],
        out_shape=[
            jax.ShapeDtypeStruct((N_GROUPS, m, LANES), bf16),
            jax.ShapeDtypeStruct((m, LANES), f32),
            jax.ShapeDtypeStruct((m, LANES), f32),
        ],
        scratch_shapes=[pltpu.VMEM((TM_IN, D_MODEL), bf16)],
        compiler_params=_params(2),
        name="inproj",
    )(x2d, norm_g, w_main, w_gate)


def _gate_kernel(gi_ref, gf_ref, ib_ref, fb_ref, a_ref, wi_ref, e_ref, wk_ref, rt_ref, dec_ref, m_ref):
    @pl.when(pl.program_id(1) == 0)
    def _():
        m_ref[...] = jnp.zeros_like(m_ref)

    ii = gi_ref[...] + ib_ref[...]
    lf = jax.nn.log_sigmoid(gf_ref[...] + fb_ref[...])
    row = lax.broadcasted_iota(jnp.int32, (LC, LANES), 0)

    bc = lf
    d = 1
    while d < LC:
        bc = bc + jnp.where(row >= d, pltpu.roll(bc, d, axis=0), 0.0)
        d *= 2
    r = ii - bc
    cm = r
    d = 1
    while d < LC:
        cm = jnp.maximum(cm, jnp.where(row >= d, pltpu.roll(cm, d, axis=0), NEG_BIG))
        d *= 2

    m_st = m_ref[0:1, :]
    mx = jnp.maximum(m_st, cm)
    a_ref[...] = -mx
    wi_ref[...] = jnp.exp(m_st - mx)
    e_ref[...] = jnp.exp(-(bc + mx))
    mx_last = mx[LC - 1:LC, :]
    wk_ref[...] = jnp.exp(r - mx_last)
    dec_ref[0] = jnp.broadcast_to(jnp.exp(m_st - mx_last), (SUBLANES, LANES))
    rt_ref[...] = r.T[0:SUBLANES, :]
    m_ref[...] = jnp.broadcast_to(bc[LC - 1:LC, :] + mx_last, (SUBLANES, LANES))


def _gate_prep(l, gi, gf, ib, fb, batch, nc):
    m = gi.shape[0]
    blk = pl.BlockSpec((LC, LANES), lambda b, c: (b * nc + c, 0))
    vec = pl.BlockSpec((None, 1, LANES), lambda b, c: (l, 0, 0))
    return pl.pallas_call(
        _gate_kernel,
        grid=(batch, nc),
        in_specs=[blk, blk, vec, vec],
        out_specs=[blk, blk, blk, blk,
                   pl.BlockSpec((SUBLANES, LC), lambda b, c: (0, b * nc + c)),
                   pl.BlockSpec((1, SUBLANES, LANES), lambda b, c: (b * nc + c, 0, 0))],
        out_shape=[jax.ShapeDtypeStruct((m, LANES), f32)] * 4
        + [jax.ShapeDtypeStruct((SUBLANES, m), f32),
           jax.ShapeDtypeStruct((batch * nc, SUBLANES, LANES), f32)],
        scratch_shapes=[pltpu.VMEM((SUBLANES, LANES), f32)],
        compiler_params=_params(2),
        name="gate_prep",
    )(gi, gf, ib, fb)


def _causal_conv_block(cur, prev_tail, w, row8):
    tail = cur[PB - HALO:PB]
    before = [pltpu.roll(jnp.where(row8 == SUBLANES - 1,
                                   prev_tail[i * SUBLANES:(i + 1) * SUBLANES],
                                   tail[i * SUBLANES:(i + 1) * SUBLANES]), 1, axis=0)
              for i in range(CONV_W - 1)]
    ext = jnp.concatenate(before + [cur], axis=0)
    acc = cur * w[CONV_W - 1:CONV_W, :]
    for j in range(1, CONV_W):
        lo = HALO - j * SUBLANES
        acc = acc + ext[lo:lo + PB] * w[CONV_W - 1 - j:CONV_W - j, :]
    return acc, tail


def _mlstm_kernel(p_ref, a_ref, wi_ref, e_ref, wk_ref, rt_ref, dec_ref, pm_ref, pmt_ref, cw_ref, hng_ref,
                  ya_ref, c_ref, n_ref, halo_ref):
    @pl.when(pl.program_id(1) == 0)
    def _():
        c_ref[...] = jnp.zeros_like(c_ref)
        n_ref[...] = jnp.zeros_like(n_ref)
        halo_ref[...] = jnp.zeros_like(halo_ref)

    row8 = lax.broadcasted_iota(jnp.int32, (SUBLANES, 2 * DK_A), 0)
    causal = (lax.broadcasted_iota(jnp.int32, (LC, LC), 0)
              >= lax.broadcasted_iota(jnp.int32, (LC, LC), 1))
    lane = lax.broadcasted_iota(jnp.int32, (1, 2 * DK_A), 1)
    qk_scale = jnp.where(lane < DK_A, 1.0, DK_A ** -0.5)
    pm = pm_ref[...]
    pmt = pmt_ref[...]

    def wide(g0):
        return jnp.concatenate([p_ref[g0], p_ref[g0 + 1]], axis=-1)

    heads = range(H_A)
    qk_seg = [jnp.dot(pm, jnp.concatenate([p_ref[G_Q + h], p_ref[G_K + h]], axis=-1),
                      preferred_element_type=f32) for h in heads]
    acts = []
    for h in heads:
        w_qk = jnp.concatenate([cw_ref[:, (G_Q + h) * LANES:(G_Q + h + 1) * LANES],
                                cw_ref[:, (G_K + h) * LANES:(G_K + h + 1) * LANES]], axis=-1)
        y, tail = _causal_conv_block(qk_seg[h], halo_ref[h], w_qk, row8)
        halo_ref[h] = tail
        acts.append((_silu(y) * qk_scale).astype(bf16))
    qk = [jnp.dot(pmt, acts[h], preferred_element_type=f32) for h in heads]
    qb = [qk[h][:, :DK_A].astype(bf16) for h in heads]
    kb = [qk[h][:, DK_A:].astype(bf16) for h in heads]
    v = [wide(G_V + 2 * h) for h in heads]

    s_raw = [lax.dot_general(qb[h], kb[h], (((1,), (1,)), ((), ())), preferred_element_type=f32)
             for h in heads]
    q_c = [jnp.dot(qb[h], c_ref[h].astype(bf16), preferred_element_type=f32) for h in heads]
    s = []
    for h in heads:
        dmat = jnp.exp(jnp.where(causal, a_ref[:, h:h + 1] + rt_ref[h:h + 1, :], NEG_BIG))
        s.append(s_raw[h] * dmat)
    s_v = [jnp.dot(s[h].astype(bf16), v[h], preferred_element_type=f32) for h in heads]

    kw = [qk[h][:, DK_A:] * wk_ref[:, h:h + 1] for h in heads]
    kw_v = [jnp.dot(kw[h].T.astype(bf16), v[h], preferred_element_type=f32) for h in heads]
    hc = []
    for h in heads:
        wi = wi_ref[:, h:h + 1]
        n_st = n_ref[h]
        dec = dec_ref[0, 0:1, h:h + 1]
        den = (jnp.sum(s[h], axis=-1, keepdims=True)
               + wi * jnp.sum(qk[h][:, :DK_A] * n_st, axis=-1, keepdims=True))
        hc.append((s_v[h] + wi * q_c[h]) * (1.0 / jnp.maximum(jnp.abs(den), e_ref[:, h:h + 1])))
        c_ref[h] = dec * c_ref[h] + kw_v[h]
        n_ref[h] = dec * n_st + jnp.sum(kw[h], axis=0, keepdims=True)

    for h in heads:
        hg = hc[h] * _sigmoid(wide(G_O + 2 * h).astype(f32))
        mu = jnp.mean(hg, axis=-1, keepdims=True)
        dev = hg - mu
        var = jnp.mean(dev * dev, axis=-1, keepdims=True)
        hn = (dev * lax.rsqrt(var + EPS)) * hng_ref[:, h * DV_A:(h + 1) * DV_A]
        ya_ref[h] = (hn * _silu(wide(G_ZA + 2 * h).astype(f32))).astype(bf16)


def _mlstm(l, p, a, wi, e, wk, rt, dec, pm, pmt, conv_w, hn_g, batch, nc):
    m = p.shape[1]
    blk = pl.BlockSpec((LC, LANES), lambda b, c: (b * nc + c, 0))
    mat = pl.BlockSpec((PB, PB), lambda b, c: (0, 0))
    return pl.pallas_call(
        _mlstm_kernel,
        grid=(batch, nc),
        in_specs=[
            pl.BlockSpec((N_GROUPS_A, LC, LANES), lambda b, c: (0, b * nc + c, 0)),
            blk, blk, blk, blk,
            pl.BlockSpec((SUBLANES, LC), lambda b, c: (0, b * nc + c)),
            pl.BlockSpec((1, SUBLANES, LANES), lambda b, c: (b * nc + c, 0, 0)),
            mat, mat,
            pl.BlockSpec((None, CONV_W, 2 * H_A * DK_A), lambda b, c: (l, 0, 0)),
            pl.BlockSpec((None, 1, H_A * DV_A), lambda b, c: (l, 0, 0)),
        ],
        out_specs=pl.BlockSpec((H_A, LC, DV_A), lambda b, c: (0, b * nc + c, 0)),
        out_shape=jax.ShapeDtypeStruct((H_A, m, DV_A), bf16),
        scratch_shapes=[
            pltpu.VMEM((H_A, DK_A, DV_A), f32),
            pltpu.VMEM((H_A, 1, DK_A), f32),
            pltpu.VMEM((H_A, HALO, 2 * DK_A), f32),
        ],
        compiler_params=_params(2),
        name="mlstm",
    )(p, a, wi, e, wk, rt, dec, pm, pmt, conv_w, hn_g)


def _lru_kernel(xb_ref, zb_ref, pm_ref, pmt_ref, cw_ref, cb_ref, wa_ref, wx_ref, ba_ref, bx_ref, lam_ref,
                yb_ref, halo_ref, carry_ref):
    @pl.when(pl.program_id(2) == 0)
    def _():
        halo_ref[...] = jnp.zeros_like(halo_ref)
        carry_ref[...] = jnp.zeros_like(carry_ref)

    n_blocks = T_LRU // PB
    row8 = lax.broadcasted_iota(jnp.int32, (SUBLANES, DB), 0)
    pm = pm_ref[...]
    pmt = pmt_ref[...]
    x = jnp.concatenate([xb_ref[0], xb_ref[1]], axis=-1)
    z = jnp.concatenate([zb_ref[0], zb_ref[1]], axis=-1)
    w = cw_ref[...]

    prev_tail = halo_ref[...]
    xc_blocks = []
    for b in range(n_blocks):
        cur = jnp.dot(pm, x[b * PB:(b + 1) * PB], preferred_element_type=f32)
        y, prev_tail = _causal_conv_block(cur, prev_tail, w, row8)
        xc_blocks.append(y + cb_ref[...])
    halo_ref[...] = prev_tail
    xc = jnp.concatenate(xc_blocks, axis=0)

    xcb = xc.astype(bf16)
    pre_r = jnp.dot(xcb, wa_ref[...], preferred_element_type=f32) + ba_ref[...]
    i = _sigmoid(jnp.dot(xcb, wx_ref[...], preferred_element_type=f32) + bx_ref[...])
    lam = lam_ref[...]
    softplus_neg = jnp.maximum(-lam, 0.0) + jnp.log1p(jnp.exp(-jnp.abs(lam)))
    half_rate = (-0.5 * LRU_C) * softplus_neg
    log_a = half_rate * jnp.tanh(0.5 * pre_r) + half_rate
    a = jnp.exp(log_a)
    u = (xc * i) * jnp.sqrt(jnp.tanh(-log_a) * (a * a + 1.0))

    carry = carry_ref[...]
    for b in range(n_blocks):
        lo = b * PB
        zp = jnp.dot(pm, z[lo:lo + PB], preferred_element_type=f32)
        gate = _silu(zp)
        h = u[lo:lo + SUBLANES]
        p = a[lo:lo + SUBLANES]
        hs, ps = [h], [p]
        for j in range(1, SEG):
            aj = a[lo + j * SUBLANES:lo + (j + 1) * SUBLANES]
            h = aj * h + u[lo + j * SUBLANES:lo + (j + 1) * SUBLANES]
            p = aj * p
            hs.append(h)
            ps.append(p)
        for d in (1, 2, 4):
            keep = row8 >= d
            h_sh = jnp.where(keep, pltpu.roll(h, d, axis=0), 0.0)
            p_sh = jnp.where(keep, pltpu.roll(p, d, axis=0), 1.0)
            h = h + p * h_sh
            p = p * p_sh
        seg_end = h + p * carry
        seg_in = jnp.where(row8 == 0, carry, pltpu.roll(seg_end, 1, axis=0))
        carry = jnp.broadcast_to(seg_end[SUBLANES - 1:SUBLANES, :], (SUBLANES, DB))
        yp = jnp.concatenate(
            [(hs[j] + ps[j] * seg_in) * gate[j * SUBLANES:(j + 1) * SUBLANES] for j in range(SEG)],
            axis=0).astype(bf16)
        yb_ref[lo:lo + PB, :] = jnp.dot(pmt, yp, preferred_element_type=f32).astype(bf16)
    carry_ref[...] = carry


def _lru(l, p, pm, pmt, conv_w, conv_b, w_a, w_x, b_a, b_x, lam, batch, nt):
    m = p.shape[1]
    vec = pl.BlockSpec((None, 1, DB), lambda b, h, t: (l, 0, h))
    mat = pl.BlockSpec((PB, PB), lambda b, h, t: (0, 0))
    gate_w = pl.BlockSpec((None, None, DB, DB), lambda b, h, t: (l, h, 0, 0))
    return pl.pallas_call(
        _lru_kernel,
        grid=(batch, H_B, nt),
        in_specs=[
            pl.BlockSpec((2, T_LRU, LANES), lambda b, h, t: (G_XB // 2 + h, b * nt + t, 0)),
            pl.BlockSpec((2, T_LRU, LANES), lambda b, h, t: (G_ZB // 2 + h, b * nt + t, 0)),
            mat, mat,
            pl.BlockSpec((None, CONV_W, DB), lambda b, h, t: (l, 0, h)),
            vec, gate_w, gate_w, vec, vec, vec,
        ],
        out_specs=pl.BlockSpec((None, T_LRU, DB), lambda b, h, t: (h, b * nt + t, 0)),
        out_shape=jax.ShapeDtypeStruct((H_B, m, DB), bf16),
        scratch_shapes=[pltpu.VMEM((HALO, DB), f32), pltpu.VMEM((SUBLANES, DB), f32)],
        compiler_params=_params(3),
        name="rglru",
    )(p, p, pm, pmt, conv_w, conv_b, w_a, w_x, b_a, b_x, lam)


def _outproj_kernel(ya_ref, yb_ref, w_ref, x_ref, o_ref):
    y = jnp.concatenate([ya_ref[h] for h in range(H_A)] + [yb_ref[h] for h in range(H_B)], axis=-1)
    o_ref[...] = x_ref[...] + jnp.dot(y, w_ref[...], preferred_element_type=f32)


def _outproj(l, ya, yb, w_out, x2d):
    m = x2d.shape[0]
    return pl.pallas_call(
        _outproj_kernel,
        grid=(D_MODEL // TN_OUT, m // TM_OUT),
        in_specs=[
            pl.BlockSpec((H_A, TM_OUT, DV_A), lambda j, i: (0, i, 0)),
            pl.BlockSpec((H_B, TM_OUT, DB), lambda j, i: (0, i, 0)),
            pl.BlockSpec((None, 2 * D_MODEL, TN_OUT), lambda j, i: (l, 0, j)),
            pl.BlockSpec((TM_OUT, TN_OUT), lambda j, i: (i, j)),
        ],
        out_specs=pl.BlockSpec((TM_OUT, TN_OUT), lambda j, i: (i, j)),
        out_shape=jax.ShapeDtypeStruct((m, D_MODEL), f32),
        compiler_params=_params(2),
        name="outproj",
    )(ya, yb, w_out, x2d)


def _norm_kernel(x_ref, g_ref, o_ref):
    x = x_ref[...]
    ms = jnp.mean(x * x, axis=-1, keepdims=True)
    o_ref[...] = (x * lax.rsqrt(ms + EPS)) * g_ref[...]


def _final_norm(x2d, g):
    m = x2d.shape[0]
    return pl.pallas_call(
        _norm_kernel,
        grid=(m // TM_NORM,),
        in_specs=[pl.BlockSpec((TM_NORM, D_MODEL), lambda i: (i, 0)),
                  pl.BlockSpec((1, D_MODEL), lambda i: (0, 0))],
        out_specs=pl.BlockSpec((TM_NORM, D_MODEL), lambda i: (i, 0)),
        out_shape=jax.ShapeDtypeStruct((m, D_MODEL), f32),
        compiler_params=_params(1),
        name="final_norm",
    )(x2d, g)


def _pad_lanes(v):
    return jnp.pad(v.astype(f32), ((0, 0), (0, LANES - v.shape[1])))[:, None, :]


def kernel(x, norm_g, w_in, i_bias, f_bias, qk_conv, head_norm_g, lru_conv_w, lru_conv_b,
           w_a, b_a, w_x, b_x, lam, w_out, final_g):
    batch, seq, d = x.shape
    m = batch * seq
    nc = seq // LC
    nt = seq // T_LRU
    n_gate0 = 4 * D_MODEL
    w_main = jnp.concatenate([w_in[:, :, :n_gate0], w_in[:, :, n_gate0 + 2 * H_A:]], axis=2).astype(bf16)
    gate_pad = ((0, 0), (0, 0), (0, LANES - H_A))
    w_gate = jnp.concatenate(
        [jnp.pad(w_in[:, :, n_gate0:n_gate0 + H_A], gate_pad),
         jnp.pad(w_in[:, :, n_gate0 + H_A:n_gate0 + 2 * H_A], gate_pad)], axis=2).astype(bf16)
    w_out_b = w_out.astype(bf16)
    w_a_b = w_a.astype(bf16)
    w_x_b = w_x.astype(bf16)
    ib = _pad_lanes(i_bias)
    fb = _pad_lanes(f_bias)
    row = lambda v: v[:, None, :]
    pm = jnp.asarray(_PM, dtype=bf16)
    pmt = jnp.asarray(_PM.T, dtype=bf16)

    x2d = x.reshape(m, d)
    for l in range(DEPTH):
        p, gi, gf = _inproj(l, x2d, row(norm_g), w_main, w_gate)
        a, wi, e, wk, rt, dec = _gate_prep(l, gi, gf, ib, fb, batch, nc)
        ya = _mlstm(l, p, a, wi, e, wk, rt, dec, pm, pmt, qk_conv, row(head_norm_g), batch, nc)
        yb = _lru(l, p, pm, pmt, lru_conv_w, row(lru_conv_b), w_a_b, w_x_b, row(b_a), row(b_x), row(lam),
                  batch, nt)
        x2d = _outproj(l, ya, yb, w_out_b, x2d)
    return _final_norm(x2d, final_g[None, :]).reshape(batch, seq, d)
```

```python
import numpy as np

import jax
import jax.numpy as jnp
from jax import lax
from jax.experimental import pallas as pl
from jax.experimental.pallas import tpu as pltpu

f32 = jnp.float32
bf16 = jnp.bfloat16

D_MODEL = 2048
DEPTH = 4
H_A = 8
DK_A = 128
DV_A = 256
H_B = 8
DB = 256
CONV_W = 4
LRU_C = 8.0
EPS = 1e-6

LANES = 128
SUBLANES = 8
N_MAIN = 6 * D_MODEL
N_IN = N_MAIN + 2 * H_A
C_GATE = 4 * D_MODEL
N_GROUPS = N_MAIN // LANES
G_Q, G_K, G_V, G_O, G_ZA, G_XB, G_ZB = 0, 8, 16, 32, 48, 64, 80
N_GROUPS_A = 64

TR_PREP = 256
TM_IN = 1024
TN_IN = 1536
PB = 256
SEG = PB // SUBLANES
LC = PB
N_CHUNKS_IN = TM_IN // LC
T_LRU = 1024
HP_LRU = 2
TM_OUT = 512
TN_OUT = 1024
TM_NORM = 1024
HALO = (CONV_W - 1) * SUBLANES

NEG_BIG = -1e30
VMEM_LIMIT = 56 * 1024 * 1024


def _segment_major_matrix():
    p = np.arange(PB)
    pm = np.zeros((PB, PB), np.float32)
    pm[p, (p % SUBLANES) * SEG + p // SUBLANES] = 1.0
    return pm


_PM = _segment_major_matrix()


def _sigmoid(x):
    return 0.5 * jnp.tanh(0.5 * x) + 0.5


def _silu(x):
    hx = 0.5 * x
    return hx * jnp.tanh(hx) + hx


def _params(n_axes):
    return pltpu.CompilerParams(dimension_semantics=("arbitrary",) * n_axes, vmem_limit_bytes=VMEM_LIMIT)


def _prep_kernel(w_ref, wm_ref, wg_ref):
    wm_ref[:, :C_GATE] = w_ref[:, :C_GATE].astype(bf16)
    wm_ref[:, C_GATE:] = w_ref[:, C_GATE + 2 * H_A:].astype(bf16)
    g = w_ref[:, C_GATE:C_GATE + LANES]
    lane = lax.broadcasted_iota(jnp.int32, (1, LANES), 1)
    wg_ref[:, :LANES] = jnp.where(lane < H_A, g, 0.0).astype(bf16)
    wg_ref[:, LANES:] = jnp.where(lane < H_A, pltpu.roll(g, LANES - H_A, axis=1), 0.0).astype(bf16)


def _prep_weights(w_in):
    return pl.pallas_call(
        _prep_kernel,
        grid=(DEPTH, D_MODEL // TR_PREP),
        in_specs=[pl.BlockSpec((None, TR_PREP, N_IN), lambda l, r: (l, r, 0))],
        out_specs=[pl.BlockSpec((None, TR_PREP, N_MAIN), lambda l, r: (l, r, 0)),
                   pl.BlockSpec((None, TR_PREP, 2 * LANES), lambda l, r: (l, r, 0))],
        out_shape=[jax.ShapeDtypeStruct((DEPTH, D_MODEL, N_MAIN), bf16),
                   jax.ShapeDtypeStruct((DEPTH, D_MODEL, 2 * LANES), bf16)],
        compiler_params=_params(2),
        name="prep_weights",
    )(w_in)


def _gate_chunk(gi, gf, ib, fb, m_st):
    ii = gi + ib
    lf = jax.nn.log_sigmoid(gf + fb)
    row = lax.broadcasted_iota(jnp.int32, (LC, LANES), 0)
    bc = lf
    d = 1
    while d < LC:
        bc = bc + jnp.where(row >= d, pltpu.roll(bc, d, axis=0), 0.0)
        d *= 2
    r = ii - bc
    cm = r
    d = 1
    while d < LC:
        cm = jnp.maximum(cm, jnp.where(row >= d, pltpu.roll(cm, d, axis=0), NEG_BIG))
        d *= 2
    mx = jnp.maximum(m_st, cm)
    mx_last = mx[LC - 1:LC, :]
    return (-mx, jnp.exp(m_st - mx), jnp.exp(-(bc + mx)), jnp.exp(r - mx_last),
            r.T[0:SUBLANES, :], jnp.exp(m_st - mx_last), bc[LC - 1:LC, :] + mx_last)


def _inproj_kernel(tiles_per_seq, x_ref, g_ref, w_ref, wg_ref, ib_ref, fb_ref,
                   p_ref, a_ref, wi_ref, e_ref, wk_ref, rt_ref, dec_ref, u_ref, gg_ref, m_ref):
    i = pl.program_id(0)
    j = pl.program_id(1)

    def matmul():
        acc = jnp.dot(u_ref[...], w_ref[...], preferred_element_type=f32)
        for g in range(TN_IN // LANES):
            p_ref[g] = acc[:, g * LANES:(g + 1) * LANES].astype(bf16)

    @pl.when(j == 0)
    def _():
        @pl.when(i % tiles_per_seq == 0)
        def _():
            m_ref[...] = jnp.zeros_like(m_ref)

        x = x_ref[...]
        ms = jnp.mean(x * x, axis=-1, keepdims=True)
        ub = ((x * lax.rsqrt(ms + EPS)) * g_ref[...]).astype(bf16)
        u_ref[...] = ub
        gg_ref[...] = jnp.dot(ub, wg_ref[...], preferred_element_type=f32)
        matmul()

    @pl.when(jnp.logical_and(j >= 1, j <= N_CHUNKS_IN))
    def _():
        c = j - 1
        rows = pl.ds(pl.multiple_of(c * LC, LC), LC)
        a, wi, e, wk, rt, dec, m_new = _gate_chunk(gg_ref[rows, :LANES], gg_ref[rows, LANES:],
                                                   ib_ref[...], fb_ref[...], m_ref[0:1, :])
        a_ref[rows, :] = a
        wi_ref[rows, :] = wi
        e_ref[rows, :] = e
        wk_ref[rows, :] = wk
        rt_ref[c] = rt
        dec_ref[c] = jnp.broadcast_to(dec, (SUBLANES, LANES))
        m_ref[...] = jnp.broadcast_to(m_new, (SUBLANES, LANES))
        matmul()

    @pl.when(j > N_CHUNKS_IN)
    def _():
        matmul()


def _inproj(l, x2d, norm_g, w_main, w_gate, ib, fb, tiles_per_seq):
    m = x2d.shape[0]
    grid = (m // TM_IN, N_MAIN // TN_IN)
    col = pl.BlockSpec((TM_IN, LANES), lambda i, j: (i, 0))
    vec = pl.BlockSpec((None, 1, LANES), lambda i, j: (l, 0, 0))
    kern = lambda *refs: _inproj_kernel(tiles_per_seq, *refs)
    return pl.pallas_call(
        kern,
        grid=grid,
        in_specs=[
            pl.BlockSpec((TM_IN, D_MODEL), lambda i, j: (i, 0)),
            pl.BlockSpec((None, 1, D_MODEL), lambda i, j: (l, 0, 0)),
            pl.BlockSpec((None, D_MODEL, TN_IN), lambda i, j: (l, 0, j)),
            pl.BlockSpec((None, D_MODEL, 2 * LANES), lambda i, j: (l, 0, 0)),
            vec, vec,
        ],
        out_specs=[
            pl.BlockSpec((TN_IN // LANES, TM_IN, LANES), lambda i, j: (j, i, 0)),
            col, col, col, col,
            pl.BlockSpec((N_CHUNKS_IN, SUBLANES, LC), lambda i, j: (i, 0, 0)),
            pl.BlockSpec((N_CHUNKS_IN, SUBLANES, LANES), lambda i, j: (i, 0, 0)),
        ],
        out_shape=[jax.ShapeDtypeStruct((N_GROUPS, m, LANES), bf16)]
        + [jax.ShapeDtypeStruct((m, LANES), f32)] * 4
        + [jax.ShapeDtypeStruct((m // LC, SUBLANES, LC), f32),
           jax.ShapeDtypeStruct((m // LC, SUBLANES, LANES), f32)],
        scratch_shapes=[pltpu.VMEM((TM_IN, D_MODEL), bf16),
                        pltpu.VMEM((TM_IN, 2 * LANES), f32),
                        pltpu.VMEM((SUBLANES, LANES), f32)],
        compiler_params=_params(2),
        name="inproj",
    )(x2d, norm_g, w_main, w_gate, ib, fb)


def _causal_conv_block(cur, prev_tail, w, row8):
    tail = cur[PB - HALO:PB]
    before = [pltpu.roll(jnp.where(row8 == SUBLANES - 1,
                                   prev_tail[i * SUBLANES:(i + 1) * SUBLANES],
                                   tail[i * SUBLANES:(i + 1) * SUBLANES]), 1, axis=0)
              for i in range(CONV_W - 1)]
    ext = jnp.concatenate(before + [cur], axis=0)
    acc = cur * w[CONV_W - 1:CONV_W, :]
    for j in range(1, CONV_W):
        lo = HALO - j * SUBLANES
        acc = acc + ext[lo:lo + PB] * w[CONV_W - 1 - j:CONV_W - j, :]
    return acc, tail


def _mlstm_kernel(p_ref, a_ref, wi_ref, e_ref, wk_ref, rt_ref, dec_ref, pm_ref, pmt_ref, cw_ref, hng_ref,
                  ya_ref, c_ref, n_ref, halo_ref):
    @pl.when(pl.program_id(1) == 0)
    def _():
        c_ref[...] = jnp.zeros_like(c_ref)
        n_ref[...] = jnp.zeros_like(n_ref)
        halo_ref[...] = jnp.zeros_like(halo_ref)

    row8 = lax.broadcasted_iota(jnp.int32, (SUBLANES, 2 * DK_A), 0)
    causal = (lax.broadcasted_iota(jnp.int32, (LC, LC), 0)
              >= lax.broadcasted_iota(jnp.int32, (LC, LC), 1))
    lane = lax.broadcasted_iota(jnp.int32, (1, 2 * DK_A), 1)
    qk_scale = jnp.where(lane < DK_A, 1.0, DK_A ** -0.5)
    pm = pm_ref[...]
    pmt = pmt_ref[...]

    def wide(g0):
        return jnp.concatenate([p_ref[g0], p_ref[g0 + 1]], axis=-1)

    heads = range(H_A)
    qk_seg = [jnp.dot(pm, jnp.concatenate([p_ref[G_Q + h], p_ref[G_K + h]], axis=-1),
                      preferred_element_type=f32) for h in heads]
    acts = []
    for h in heads:
        w_qk = jnp.concatenate([cw_ref[:, (G_Q + h) * LANES:(G_Q + h + 1) * LANES],
                                cw_ref[:, (G_K + h) * LANES:(G_K + h + 1) * LANES]], axis=-1)
        y, tail = _causal_conv_block(qk_seg[h], halo_ref[h], w_qk, row8)
        halo_ref[h] = tail
        acts.append((_silu(y) * qk_scale).astype(bf16))
    qk = [jnp.dot(pmt, acts[h], preferred_element_type=f32) for h in heads]
    qb = [qk[h][:, :DK_A].astype(bf16) for h in heads]
    kb = [qk[h][:, DK_A:].astype(bf16) for h in heads]
    v = [wide(G_V + 2 * h) for h in heads]

    s_raw = [lax.dot_general(qb[h], kb[h], (((1,), (1,)), ((), ())), preferred_element_type=f32)
             for h in heads]
    q_c = [jnp.dot(qb[h], c_ref[h].astype(bf16), preferred_element_type=f32) for h in heads]
    s = []
    for h in heads:
        dmat = jnp.exp(jnp.where(causal, a_ref[:, h:h + 1] + rt_ref[h:h + 1, :], NEG_BIG))
        s.append(s_raw[h] * dmat)
    s_v = [jnp.dot(s[h].astype(bf16), v[h], preferred_element_type=f32) for h in heads]

    kw = [qk[h][:, DK_A:] * wk_ref[:, h:h + 1] for h in heads]
    kw_v = [jnp.dot(kw[h].T.astype(bf16), v[h], preferred_element_type=f32) for h in heads]
    hc = []
    for h in heads:
        wi = wi_ref[:, h:h + 1]
        n_st = n_ref[h]
        dec = dec_ref[0:1, h:h + 1]
        den = (jnp.sum(s[h], axis=-1, keepdims=True)
               + wi * jnp.sum(qk[h][:, :DK_A] * n_st, axis=-1, keepdims=True))
        hc.append((s_v[h] + wi * q_c[h]) * (1.0 / jnp.maximum(jnp.abs(den), e_ref[:, h:h + 1])))
        c_ref[h] = dec * c_ref[h] + kw_v[h]
        n_ref[h] = dec * n_st + jnp.sum(kw[h], axis=0, keepdims=True)

    for h in heads:
        hg = hc[h] * _sigmoid(wide(G_O + 2 * h).astype(f32))
        mu = jnp.mean(hg, axis=-1, keepdims=True)
        dev = hg - mu
        var = jnp.mean(dev * dev, axis=-1, keepdims=True)
        hn = (dev * lax.rsqrt(var + EPS)) * hng_ref[:, h * DV_A:(h + 1) * DV_A]
        ya_ref[h] = (hn * _silu(wide(G_ZA + 2 * h).astype(f32))).astype(bf16)


def _mlstm(l, p, a, wi, e, wk, rt, dec, pm, pmt, conv_w, hn_g, batch, nc):
    m = p.shape[1]
    blk = pl.BlockSpec((LC, LANES), lambda b, c: (b * nc + c, 0))
    mat = pl.BlockSpec((PB, PB), lambda b, c: (0, 0))
    return pl.pallas_call(
        _mlstm_kernel,
        grid=(batch, nc),
        in_specs=[
            pl.BlockSpec((N_GROUPS_A, LC, LANES), lambda b, c: (0, b * nc + c, 0)),
            blk, blk, blk, blk,
            pl.BlockSpec((None, SUBLANES, LC), lambda b, c: (b * nc + c, 0, 0)),
            pl.BlockSpec((None, SUBLANES, LANES), lambda b, c: (b * nc + c, 0, 0)),
            mat, mat,
            pl.BlockSpec((None, CONV_W, 2 * H_A * DK_A), lambda b, c: (l, 0, 0)),
            pl.BlockSpec((None, 1, H_A * DV_A), lambda b, c: (l, 0, 0)),
        ],
        out_specs=pl.BlockSpec((H_A, LC, DV_A), lambda b, c: (0, b * nc + c, 0)),
        out_shape=jax.ShapeDtypeStruct((H_A, m, DV_A), bf16),
        scratch_shapes=[
            pltpu.VMEM((H_A, DK_A, DV_A), f32),
            pltpu.VMEM((H_A, 1, DK_A), f32),
            pltpu.VMEM((H_A, HALO, 2 * DK_A), f32),
        ],
        compiler_params=_params(2),
        name="mlstm",
    )(p, a, wi, e, wk, rt, dec, pm, pmt, conv_w, hn_g)


def _lru_kernel(xb_ref, zb_ref, pm_ref, pmt_ref, cw_ref, cb_ref, wa_ref, wx_ref, ba_ref, bx_ref, lam_ref,
                yb_ref, halo_ref, carry_ref):
    @pl.when(pl.program_id(2) == 0)
    def _():
        halo_ref[...] = jnp.zeros_like(halo_ref)
        carry_ref[...] = jnp.zeros_like(carry_ref)

    n_blocks = T_LRU // PB
    heads = range(HP_LRU)
    row8 = lax.broadcasted_iota(jnp.int32, (SUBLANES, DB), 0)
    pm = pm_ref[...]
    pmt = pmt_ref[...]

    def wide(ref, k):
        return jnp.concatenate([ref[2 * k], ref[2 * k + 1]], axis=-1)

    def chan(ref, k):
        return ref[:, k * DB:(k + 1) * DB]

    x_seg = [[jnp.dot(pm, wide(xb_ref, k)[b * PB:(b + 1) * PB], preferred_element_type=f32)
              for b in range(n_blocks)] for k in heads]
    z_seg = [[jnp.dot(pm, wide(zb_ref, k)[b * PB:(b + 1) * PB], preferred_element_type=f32)
              for b in range(n_blocks)] for k in heads]

    xc = []
    for k in heads:
        prev_tail = halo_ref[k]
        blocks = []
        for b in range(n_blocks):
            y, prev_tail = _causal_conv_block(x_seg[k][b], prev_tail, chan(cw_ref, k), row8)
            blocks.append(y + chan(cb_ref, k))
        halo_ref[k] = prev_tail
        xc.append(jnp.concatenate(blocks, axis=0))

    xcb = [xc[k].astype(bf16) for k in heads]
    pre_r = [jnp.dot(xcb[k], wa_ref[k], preferred_element_type=f32) + chan(ba_ref, k) for k in heads]
    pre_i = [jnp.dot(xcb[k], wx_ref[k], preferred_element_type=f32) + chan(bx_ref, k) for k in heads]
    a, u = [], []
    for k in heads:
        lam = chan(lam_ref, k)
        softplus_neg = jnp.maximum(-lam, 0.0) + jnp.log1p(jnp.exp(-jnp.abs(lam)))
        half_rate = (-0.5 * LRU_C) * softplus_neg
        log_a = half_rate * jnp.tanh(0.5 * pre_r[k]) + half_rate
        ak = jnp.exp(log_a)
        a.append(ak)
        u.append((xc[k] * _sigmoid(pre_i[k])) * jnp.sqrt(jnp.tanh(-log_a) * (ak * ak + 1.0)))

    carry = [carry_ref[k] for k in heads]
    for b in range(n_blocks):
        lo = b * PB
        yp = []
        for k in heads:
            gate = _silu(z_seg[k][b])
            h = u[k][lo:lo + SUBLANES]
            p = a[k][lo:lo + SUBLANES]
            hs, ps = [h], [p]
            for j in range(1, SEG):
                aj = a[k][lo + j * SUBLANES:lo + (j + 1) * SUBLANES]
                h = aj * h + u[k][lo + j * SUBLANES:lo + (j + 1) * SUBLANES]
                p = aj * p
                hs.append(h)
                ps.append(p)
            for d in (1, 2, 4):
                keep = row8 >= d
                h_sh = jnp.where(keep, pltpu.roll(h, d, axis=0), 0.0)
                p_sh = jnp.where(keep, pltpu.roll(p, d, axis=0), 1.0)
                h = h + p * h_sh
                p = p * p_sh
            seg_end = h + p * carry[k]
            seg_in = jnp.where(row8 == 0, carry[k], pltpu.roll(seg_end, 1, axis=0))
            carry[k] = jnp.broadcast_to(seg_end[SUBLANES - 1:SUBLANES, :], (SUBLANES, DB))
            yp.append(jnp.concatenate(
                [(hs[j] + ps[j] * seg_in) * gate[j * SUBLANES:(j + 1) * SUBLANES] for j in range(SEG)],
                axis=0).astype(bf16))
        for k in heads:
            yb_ref[k, lo:lo + PB, :] = jnp.dot(pmt, yp[k], preferred_element_type=f32).astype(bf16)
    for k in heads:
        carry_ref[k] = carry[k]


def _lru(l, p, pm, pmt, conv_w, conv_b, w_a, w_x, b_a, b_x, lam, batch, nt):
    m = p.shape[1]
    width = HP_LRU * DB
    groups = width // LANES
    vec = pl.BlockSpec((None, 1, width), lambda b, h, t: (l, 0, h))
    mat = pl.BlockSpec((PB, PB), lambda b, h, t: (0, 0))
    gate_w = pl.BlockSpec((None, HP_LRU, DB, DB), lambda b, h, t: (l, h, 0, 0))
    return pl.pallas_call(
        _lru_kernel,
        grid=(batch, H_B // HP_LRU, nt),
        in_specs=[
            pl.BlockSpec((groups, T_LRU, LANES), lambda b, h, t: (G_XB // groups + h, b * nt + t, 0)),
            pl.BlockSpec((groups, T_LRU, LANES), lambda b, h, t: (G_ZB // groups + h, b * nt + t, 0)),
            mat, mat,
            pl.BlockSpec((None, CONV_W, width), lambda b, h, t: (l, 0, h)),
            vec, gate_w, gate_w, vec, vec, vec,
        ],
        out_specs=pl.BlockSpec((HP_LRU, T_LRU, DB), lambda b, h, t: (h, b * nt + t, 0)),
        out_shape=jax.ShapeDtypeStruct((H_B, m, DB), bf16),
        scratch_shapes=[pltpu.VMEM((HP_LRU, HALO, DB), f32), pltpu.VMEM((HP_LRU, SUBLANES, DB), f32)],
        compiler_params=_params(3),
        name="rglru",
    )(p, p, pm, pmt, conv_w, conv_b, w_a, w_x, b_a, b_x, lam)


def _outproj_kernel(ya_ref, yb_ref, w_ref, x_ref, o_ref):
    y = jnp.concatenate([ya_ref[h] for h in range(H_A)] + [yb_ref[h] for h in range(H_B)], axis=-1)
    o_ref[...] = x_ref[...] + jnp.dot(y, w_ref[...], preferred_element_type=f32)


def _outproj(l, ya, yb, w_out, x2d):
    m = x2d.shape[0]
    return pl.pallas_call(
        _outproj_kernel,
        grid=(D_MODEL // TN_OUT, m // TM_OUT),
        in_specs=[
            pl.BlockSpec((H_A, TM_OUT, DV_A), lambda j, i: (0, i, 0)),
            pl.BlockSpec((H_B, TM_OUT, DB), lambda j, i: (0, i, 0)),
            pl.BlockSpec((None, 2 * D_MODEL, TN_OUT), lambda j, i: (l, 0, j)),
            pl.BlockSpec((TM_OUT, TN_OUT), lambda j, i: (i, j)),
        ],
        out_specs=pl.BlockSpec((TM_OUT, TN_OUT), lambda j, i: (i, j)),
        out_shape=jax.ShapeDtypeStruct((m, D_MODEL), f32),
        compiler_params=_params(2),
        name="outproj",
    )(ya, yb, w_out, x2d)


def _norm_kernel(x_ref, g_ref, o_ref):
    x = x_ref[...]
    ms = jnp.mean(x * x, axis=-1, keepdims=True)
    o_ref[...] = (x * lax.rsqrt(ms + EPS)) * g_ref[...]


def _final_norm(x2d, g):
    m = x2d.shape[0]
    return pl.pallas_call(
        _norm_kernel,
        grid=(m // TM_NORM,),
        in_specs=[pl.BlockSpec((TM_NORM, D_MODEL), lambda i: (i, 0)),
                  pl.BlockSpec((1, D_MODEL), lambda i: (0, 0))],
        out_specs=pl.BlockSpec((TM_NORM, D_MODEL), lambda i: (i, 0)),
        out_shape=jax.ShapeDtypeStruct((m, D_MODEL), f32),
        compiler_params=_params(1),
        name="final_norm",
    )(x2d, g)


def _pad_lanes(v):
    return jnp.pad(v.astype(f32), ((0, 0), (0, LANES - v.shape[1])))[:, None, :]


def kernel(x, norm_g, w_in, i_bias, f_bias, qk_conv, head_norm_g, lru_conv_w, lru_conv_b,
           w_a, b_a, w_x, b_x, lam, w_out, final_g):
    batch, seq, d = x.shape
    m = batch * seq
    nc = seq // LC
    nt = seq // T_LRU
    w_main, w_gate = _prep_weights(w_in)
    w_out_b = w_out.astype(bf16)
    w_a_b = w_a.astype(bf16)
    w_x_b = w_x.astype(bf16)
    ib = _pad_lanes(i_bias)
    fb = _pad_lanes(f_bias)
    row = lambda v: v[:, None, :]
    pm = jnp.asarray(_PM, dtype=bf16)
    pmt = jnp.asarray(_PM.T, dtype=bf16)

    x2d = x.reshape(m, d)
    for l in range(DEPTH):
        p, a, wi, e, wk, rt, dec = _inproj(l, x2d, row(norm_g), w_main, w_gate, ib, fb, seq // TM_IN)
        ya = _mlstm(l, p, a, wi, e, wk, rt, dec, pm, pmt, qk_conv, row(head_norm_g), batch, nc)
        yb = _lru(l, p, pm, pmt, lru_conv_w, row(lru_conv_b), w_a_b, w_x_b, row(b_a), row(b_x), row(lam),
                  batch, nt)
        x2d = _outproj(l, ya, yb, w_out_b, x2d)
    return _final_norm(x2d, final_g[None, :]).reshape(batch, seq, d)
```

```python
import numpy as np

import jax
import jax.numpy as jnp
from jax import lax
from jax.experimental import pallas as pl
from jax.experimental.pallas import tpu as pltpu

f32 = jnp.float32
bf16 = jnp.bfloat16

D_MODEL = 2048
DEPTH = 4
H_A = 8
DK_A = 128
DV_A = 256
H_B = 8
DB = 256
CONV_W = 4
LRU_C = 8.0
EPS = 1e-6

LANES = 128
SUBLANES = 8
N_MAIN = 6 * D_MODEL
N_IN = N_MAIN + 2 * H_A
C_GATE = 4 * D_MODEL
N_GROUPS = N_MAIN // LANES
G_Q, G_K, G_V, G_O, G_ZA, G_XB, G_ZB = 0, 8, 16, 32, 48, 64, 80
N_GROUPS_A = 64

N_GATE = 2 * H_A
TQ_PREP = 1024
Q_GATE = C_GATE // TQ_PREP
TM_IN = 1024
TN_IN = 1536
PB = 256
SEG = PB // SUBLANES
LC = PB
N_CHUNKS_IN = TM_IN // LC
T_LRU = 1024
HP_LRU = 2
TM_OUT = 512
TN_OUT = 1024
TM_NORM = 1024
HALO = (CONV_W - 1) * SUBLANES

NEG_BIG = -1e30
VMEM_LIMIT = 56 * 1024 * 1024


def _segment_major_matrix():
    p = np.arange(PB)
    pm = np.zeros((PB, PB), np.float32)
    pm[p, (p % SUBLANES) * SEG + p // SUBLANES] = 1.0
    return pm


_PM = _segment_major_matrix()


def _sigmoid(x):
    return 0.5 * jnp.tanh(0.5 * x) + 0.5


def _silu(x):
    hx = 0.5 * x
    return hx * jnp.tanh(hx) + hx


def _params(n_axes):
    return pltpu.CompilerParams(dimension_semantics=("arbitrary",) * n_axes, vmem_limit_bytes=VMEM_LIMIT)


def _prep_kernel(wt_ref, nxt_ref, wm_ref, wg_ref):
    q = pl.program_id(1)

    @pl.when(q < Q_GATE)
    def _():
        wm_ref[...] = wt_ref[...].T.astype(bf16)

    @pl.when(q >= Q_GATE)
    def _():
        wm_ref[...] = jnp.concatenate([wt_ref[N_GATE:TQ_PREP, :], nxt_ref[...]], axis=0).T.astype(bf16)

    @pl.when(q == Q_GATE)
    def _():
        zeros = jnp.zeros((LANES - H_A, D_MODEL), f32)
        wg_ref[...] = jnp.concatenate([wt_ref[0:H_A, :], zeros, wt_ref[H_A:N_GATE, :], zeros],
                                      axis=0).T.astype(bf16)


def _prep_weights(w_in_t):
    nxt_block = lambda l, q: (l, jnp.maximum(q + 1 - Q_GATE, 0) * (TQ_PREP // N_GATE) + C_GATE // N_GATE, 0)
    return pl.pallas_call(
        _prep_kernel,
        grid=(DEPTH, N_MAIN // TQ_PREP),
        in_specs=[pl.BlockSpec((None, TQ_PREP, D_MODEL), lambda l, q: (l, q, 0)),
                  pl.BlockSpec((None, N_GATE, D_MODEL), nxt_block)],
        out_specs=[pl.BlockSpec((None, D_MODEL, TQ_PREP), lambda l, q: (l, 0, q)),
                   pl.BlockSpec((None, D_MODEL, 2 * LANES), lambda l, q: (l, 0, 0))],
        out_shape=[jax.ShapeDtypeStruct((DEPTH, D_MODEL, N_MAIN), bf16),
                   jax.ShapeDtypeStruct((DEPTH, D_MODEL, 2 * LANES), bf16)],
        compiler_params=_params(2),
        name="prep_weights",
    )(w_in_t, w_in_t)


def _gate_chunk(gi, gf, ib, fb, m_st):
    ii = gi + ib
    lf = jax.nn.log_sigmoid(gf + fb)
    row = lax.broadcasted_iota(jnp.int32, (LC, LANES), 0)
    bc = lf
    d = 1
    while d < LC:
        bc = bc + jnp.where(row >= d, pltpu.roll(bc, d, axis=0), 0.0)
        d *= 2
    r = ii - bc
    cm = r
    d = 1
    while d < LC:
        cm = jnp.maximum(cm, jnp.where(row >= d, pltpu.roll(cm, d, axis=0), NEG_BIG))
        d *= 2
    mx = jnp.maximum(m_st, cm)
    mx_last = mx[LC - 1:LC, :]
    return (-mx, jnp.exp(m_st - mx), jnp.exp(-(bc + mx)), jnp.exp(r - mx_last),
            r.T[0:SUBLANES, :], jnp.exp(m_st - mx_last), bc[LC - 1:LC, :] + mx_last)


def _inproj_kernel(tiles_per_seq, x_ref, g_ref, w_ref, wg_ref, ib_ref, fb_ref,
                   p_ref, a_ref, wi_ref, e_ref, wk_ref, rt_ref, dec_ref, u_ref, gg_ref, m_ref):
    i = pl.program_id(0)
    j = pl.program_id(1)

    def matmul():
        acc = jnp.dot(u_ref[...], w_ref[...], preferred_element_type=f32)
        for g in range(TN_IN // LANES):
            p_ref[g] = acc[:, g * LANES:(g + 1) * LANES].astype(bf16)

    @pl.when(j == 0)
    def _():
        @pl.when(i % tiles_per_seq == 0)
        def _():
            m_ref[...] = jnp.zeros_like(m_ref)

        x = x_ref[...]
        ms = jnp.mean(x * x, axis=-1, keepdims=True)
        ub = ((x * lax.rsqrt(ms + EPS)) * g_ref[...]).astype(bf16)
        u_ref[...] = ub
        gg_ref[...] = jnp.dot(ub, wg_ref[...], preferred_element_type=f32)
        matmul()

    @pl.when(jnp.logical_and(j >= 1, j <= N_CHUNKS_IN))
    def _():
        c = j - 1
        rows = pl.ds(pl.multiple_of(c * LC, LC), LC)
        a, wi, e, wk, rt, dec, m_new = _gate_chunk(gg_ref[rows, :LANES], gg_ref[rows, LANES:],
                                                   ib_ref[...], fb_ref[...], m_ref[0:1, :])
        a_ref[rows, :] = a
        wi_ref[rows, :] = wi
        e_ref[rows, :] = e
        wk_ref[rows, :] = wk
        rt_ref[c] = rt
        dec_ref[c] = jnp.broadcast_to(dec, (SUBLANES, LANES))
        m_ref[...] = jnp.broadcast_to(m_new, (SUBLANES, LANES))
        matmul()

    @pl.when(j > N_CHUNKS_IN)
    def _():
        matmul()


def _inproj(l, x2d, norm_g, w_main, w_gate, ib, fb, tiles_per_seq):
    m = x2d.shape[0]
    grid = (m // TM_IN, N_MAIN // TN_IN)
    col = pl.BlockSpec((TM_IN, LANES), lambda i, j: (i, 0))
    vec = pl.BlockSpec((None, 1, LANES), lambda i, j: (l, 0, 0))
    kern = lambda *refs: _inproj_kernel(tiles_per_seq, *refs)
    return pl.pallas_call(
        kern,
        grid=grid,
        in_specs=[
            pl.BlockSpec((TM_IN, D_MODEL), lambda i, j: (i, 0)),
            pl.BlockSpec((None, 1, D_MODEL), lambda i, j: (l, 0, 0)),
            pl.BlockSpec((None, D_MODEL, TN_IN), lambda i, j: (l, 0, j)),
            pl.BlockSpec((None, D_MODEL, 2 * LANES), lambda i, j: (l, 0, 0)),
            vec, vec,
        ],
        out_specs=[
            pl.BlockSpec((TN_IN // LANES, TM_IN, LANES), lambda i, j: (j, i, 0)),
            col, col, col, col,
            pl.BlockSpec((N_CHUNKS_IN, SUBLANES, LC), lambda i, j: (i, 0, 0)),
            pl.BlockSpec((N_CHUNKS_IN, SUBLANES, LANES), lambda i, j: (i, 0, 0)),
        ],
        out_shape=[jax.ShapeDtypeStruct((N_GROUPS, m, LANES), bf16)]
        + [jax.ShapeDtypeStruct((m, LANES), f32)] * 4
        + [jax.ShapeDtypeStruct((m // LC, SUBLANES, LC), f32),
           jax.ShapeDtypeStruct((m // LC, SUBLANES, LANES), f32)],
        scratch_shapes=[pltpu.VMEM((TM_IN, D_MODEL), bf16),
                        pltpu.VMEM((TM_IN, 2 * LANES), f32),
                        pltpu.VMEM((SUBLANES, LANES), f32)],
        compiler_params=_params(2),
        name="inproj",
    )(x2d, norm_g, w_main, w_gate, ib, fb)


def _causal_conv_block(cur, prev_tail, w, row8):
    tail = cur[PB - HALO:PB]
    before = [pltpu.roll(jnp.where(row8 == SUBLANES - 1,
                                   prev_tail[i * SUBLANES:(i + 1) * SUBLANES],
                                   tail[i * SUBLANES:(i + 1) * SUBLANES]), 1, axis=0)
              for i in range(CONV_W - 1)]
    ext = jnp.concatenate(before + [cur], axis=0)
    acc = cur * w[CONV_W - 1:CONV_W, :]
    for j in range(1, CONV_W):
        lo = HALO - j * SUBLANES
        acc = acc + ext[lo:lo + PB] * w[CONV_W - 1 - j:CONV_W - j, :]
    return acc, tail


def _mlstm_kernel(p_ref, a_ref, wi_ref, e_ref, wk_ref, rt_ref, dec_ref, pm_ref, pmt_ref, cw_ref, hng_ref,
                  ya_ref, c_ref, n_ref, halo_ref):
    @pl.when(pl.program_id(1) == 0)
    def _():
        c_ref[...] = jnp.zeros_like(c_ref)
        n_ref[...] = jnp.zeros_like(n_ref)
        halo_ref[...] = jnp.zeros_like(halo_ref)

    row8 = lax.broadcasted_iota(jnp.int32, (SUBLANES, 2 * DK_A), 0)
    causal = (lax.broadcasted_iota(jnp.int32, (LC, LC), 0)
              >= lax.broadcasted_iota(jnp.int32, (LC, LC), 1))
    lane = lax.broadcasted_iota(jnp.int32, (1, 2 * DK_A), 1)
    qk_scale = jnp.where(lane < DK_A, 1.0, DK_A ** -0.5)
    pm = pm_ref[...]
    pmt = pmt_ref[...]

    def wide(g0):
        return jnp.concatenate([p_ref[g0], p_ref[g0 + 1]], axis=-1)

    heads = range(H_A)
    qk_seg = [jnp.dot(pm, jnp.concatenate([p_ref[G_Q + h], p_ref[G_K + h]], axis=-1),
                      preferred_element_type=f32) for h in heads]
    acts = []
    for h in heads:
        w_qk = jnp.concatenate([cw_ref[:, (G_Q + h) * LANES:(G_Q + h + 1) * LANES],
                                cw_ref[:, (G_K + h) * LANES:(G_K + h + 1) * LANES]], axis=-1)
        y, tail = _causal_conv_block(qk_seg[h], halo_ref[h], w_qk, row8)
        halo_ref[h] = tail
        acts.append((_silu(y) * qk_scale).astype(bf16))
    qk = [jnp.dot(pmt, acts[h], preferred_element_type=f32) for h in heads]
    qb = [qk[h][:, :DK_A].astype(bf16) for h in heads]
    kb = [qk[h][:, DK_A:].astype(bf16) for h in heads]
    v = [wide(G_V + 2 * h) for h in heads]

    s_raw = [lax.dot_general(qb[h], kb[h], (((1,), (1,)), ((), ())), preferred_element_type=f32)
             for h in heads]
    q_c = [jnp.dot(qb[h], c_ref[h].astype(bf16), preferred_element_type=f32) for h in heads]
    s = []
    for h in heads:
        dmat = jnp.exp(jnp.where(causal, a_ref[:, h:h + 1] + rt_ref[h:h + 1, :], NEG_BIG))
        s.append(s_raw[h] * dmat)
    s_v = [jnp.dot(s[h].astype(bf16), v[h], preferred_element_type=f32) for h in heads]

    kw = [qk[h][:, DK_A:] * wk_ref[:, h:h + 1] for h in heads]
    kw_v = [jnp.dot(kw[h].T.astype(bf16), v[h], preferred_element_type=f32) for h in heads]
    hc = []
    for h in heads:
        wi = wi_ref[:, h:h + 1]
        n_st = n_ref[h]
        dec = dec_ref[0:1, h:h + 1]
        den = (jnp.sum(s[h], axis=-1, keepdims=True)
               + wi * jnp.sum(qk[h][:, :DK_A] * n_st, axis=-1, keepdims=True))
        hc.append((s_v[h] + wi * q_c[h]) * (1.0 / jnp.maximum(jnp.abs(den), e_ref[:, h:h + 1])))
        c_ref[h] = dec * c_ref[h] + kw_v[h]
        n_ref[h] = dec * n_st + jnp.sum(kw[h], axis=0, keepdims=True)

    for h in heads:
        hg = hc[h] * _sigmoid(wide(G_O + 2 * h).astype(f32))
        mu = jnp.mean(hg, axis=-1, keepdims=True)
        dev = hg - mu
        var = jnp.mean(dev * dev, axis=-1, keepdims=True)
        hn = (dev * lax.rsqrt(var + EPS)) * hng_ref[:, h * DV_A:(h + 1) * DV_A]
        ya_ref[h] = (hn * _silu(wide(G_ZA + 2 * h).astype(f32))).astype(bf16)


def _mlstm(l, p, a, wi, e, wk, rt, dec, pm, pmt, conv_w, hn_g, batch, nc):
    m = p.shape[1]
    blk = pl.BlockSpec((LC, LANES), lambda b, c: (b * nc + c, 0))
    mat = pl.BlockSpec((PB, PB), lambda b, c: (0, 0))
    return pl.pallas_call(
        _mlstm_kernel,
        grid=(batch, nc),
        in_specs=[
            pl.BlockSpec((N_GROUPS_A, LC, LANES), lambda b, c: (0, b * nc + c, 0)),
            blk, blk, blk, blk,
            pl.BlockSpec((None, SUBLANES, LC), lambda b, c: (b * nc + c, 0, 0)),
            pl.BlockSpec((None, SUBLANES, LANES), lambda b, c: (b * nc + c, 0, 0)),
            mat, mat,
            pl.BlockSpec((None, CONV_W, 2 * H_A * DK_A), lambda b, c: (l, 0, 0)),
            pl.BlockSpec((None, 1, H_A * DV_A), lambda b, c: (l, 0, 0)),
        ],
        out_specs=pl.BlockSpec((H_A, LC, DV_A), lambda b, c: (0, b * nc + c, 0)),
        out_shape=jax.ShapeDtypeStruct((H_A, m, DV_A), bf16),
        scratch_shapes=[
            pltpu.VMEM((H_A, DK_A, DV_A), f32),
            pltpu.VMEM((H_A, 1, DK_A), f32),
            pltpu.VMEM((H_A, HALO, 2 * DK_A), f32),
        ],
        compiler_params=_params(2),
        name="mlstm",
    )(p, a, wi, e, wk, rt, dec, pm, pmt, conv_w, hn_g)


def _lru_kernel(xb_ref, zb_ref, pm_ref, pmt_ref, cw_ref, cb_ref, wa_ref, wx_ref, ba_ref, bx_ref, lam_ref,
                yb_ref, halo_ref, carry_ref):
    @pl.when(pl.program_id(2) == 0)
    def _():
        halo_ref[...] = jnp.zeros_like(halo_ref)
        carry_ref[...] = jnp.zeros_like(carry_ref)

    n_blocks = T_LRU // PB
    heads = range(HP_LRU)
    row8 = lax.broadcasted_iota(jnp.int32, (SUBLANES, DB), 0)
    pm = pm_ref[...]
    pmt = pmt_ref[...]

    def wide(ref, k):
        return jnp.concatenate([ref[2 * k], ref[2 * k + 1]], axis=-1)

    def chan(ref, k):
        return ref[:, k * DB:(k + 1) * DB]

    x_seg = [[jnp.dot(pm, wide(xb_ref, k)[b * PB:(b + 1) * PB], preferred_element_type=f32)
              for b in range(n_blocks)] for k in heads]
    z_seg = [[jnp.dot(pm, wide(zb_ref, k)[b * PB:(b + 1) * PB], preferred_element_type=f32)
              for b in range(n_blocks)] for k in heads]

    xc = []
    for k in heads:
        prev_tail = halo_ref[k]
        blocks = []
        for b in range(n_blocks):
            y, prev_tail = _causal_conv_block(x_seg[k][b], prev_tail, chan(cw_ref, k), row8)
            blocks.append(y + chan(cb_ref, k))
        halo_ref[k] = prev_tail
        xc.append(jnp.concatenate(blocks, axis=0))

    xcb = [xc[k].astype(bf16) for k in heads]
    pre_r = [jnp.dot(xcb[k], wa_ref[k], preferred_element_type=f32) + chan(ba_ref, k) for k in heads]
    pre_i = [jnp.dot(xcb[k], wx_ref[k], preferred_element_type=f32) + chan(bx_ref, k) for k in heads]
    a, u = [], []
    for k in heads:
        lam = chan(lam_ref, k)
        softplus_neg = jnp.maximum(-lam, 0.0) + jnp.log1p(jnp.exp(-jnp.abs(lam)))
        half_rate = (-0.5 * LRU_C) * softplus_neg
        log_a = half_rate * jnp.tanh(0.5 * pre_r[k]) + half_rate
        ak = jnp.exp(log_a)
        a.append(ak)
        u.append((xc[k] * _sigmoid(pre_i[k])) * jnp.sqrt(jnp.tanh(-log_a) * (ak * ak + 1.0)))

    carry = [carry_ref[k] for k in heads]
    for b in range(n_blocks):
        lo = b * PB
        yp = []
        for k in heads:
            gate = _silu(z_seg[k][b])
            h = u[k][lo:lo + SUBLANES]
            p = a[k][lo:lo + SUBLANES]
            hs, ps = [h], [p]
            for j in range(1, SEG):
                aj = a[k][lo + j * SUBLANES:lo + (j + 1) * SUBLANES]
                h = aj * h + u[k][lo + j * SUBLANES:lo + (j + 1) * SUBLANES]
                p = aj * p
                hs.append(h)
                ps.append(p)
            for d in (1, 2, 4):
                keep = row8 >= d
                h_sh = jnp.where(keep, pltpu.roll(h, d, axis=0), 0.0)
                p_sh = jnp.where(keep, pltpu.roll(p, d, axis=0), 1.0)
                h = h + p * h_sh
                p = p * p_sh
            seg_end = h + p * carry[k]
            seg_in = jnp.where(row8 == 0, carry[k], pltpu.roll(seg_end, 1, axis=0))
            carry[k] = jnp.broadcast_to(seg_end[SUBLANES - 1:SUBLANES, :], (SUBLANES, DB))
            yp.append(jnp.concatenate(
                [(hs[j] + ps[j] * seg_in) * gate[j * SUBLANES:(j + 1) * SUBLANES] for j in range(SEG)],
                axis=0).astype(bf16))
        for k in heads:
            yb_ref[k, lo:lo + PB, :] = jnp.dot(pmt, yp[k], preferred_element_type=f32).astype(bf16)
    for k in heads:
        carry_ref[k] = carry[k]


def _lru(l, p, pm, pmt, conv_w, conv_b, w_a, w_x, b_a, b_x, lam, batch, nt):
    m = p.shape[1]
    width = HP_LRU * DB
    groups = width // LANES
    vec = pl.BlockSpec((None, 1, width), lambda b, h, t: (l, 0, h))
    mat = pl.BlockSpec((PB, PB), lambda b, h, t: (0, 0))
    gate_w = pl.BlockSpec((None, HP_LRU, DB, DB), lambda b, h, t: (l, h, 0, 0))
    return pl.pallas_call(
        _lru_kernel,
        grid=(batch, H_B // HP_LRU, nt),
        in_specs=[
            pl.BlockSpec((groups, T_LRU, LANES), lambda b, h, t: (G_XB // groups + h, b * nt + t, 0)),
            pl.BlockSpec((groups, T_LRU, LANES), lambda b, h, t: (G_ZB // groups + h, b * nt + t, 0)),
            mat, mat,
            pl.BlockSpec((None, CONV_W, width), lambda b, h, t: (l, 0, h)),
            vec, gate_w, gate_w, vec, vec, vec,
        ],
        out_specs=pl.BlockSpec((HP_LRU, T_LRU, DB), lambda b, h, t: (h, b * nt + t, 0)),
        out_shape=jax.ShapeDtypeStruct((H_B, m, DB), bf16),
        scratch_shapes=[pltpu.VMEM((HP_LRU, HALO, DB), f32), pltpu.VMEM((HP_LRU, SUBLANES, DB), f32)],
        compiler_params=_params(3),
        name="rglru",
    )(p, p, pm, pmt, conv_w, conv_b, w_a, w_x, b_a, b_x, lam)


def _outproj_kernel(ya_ref, yb_ref, w_ref, x_ref, o_ref):
    y = jnp.concatenate([ya_ref[h] for h in range(H_A)] + [yb_ref[h] for h in range(H_B)], axis=-1)
    o_ref[...] = x_ref[...] + jnp.dot(y, w_ref[...], preferred_element_type=f32)


def _outproj(l, ya, yb, w_out, x2d):
    m = x2d.shape[0]
    return pl.pallas_call(
        _outproj_kernel,
        grid=(D_MODEL // TN_OUT, m // TM_OUT),
        in_specs=[
            pl.BlockSpec((H_A, TM_OUT, DV_A), lambda j, i: (0, i, 0)),
            pl.BlockSpec((H_B, TM_OUT, DB), lambda j, i: (0, i, 0)),
            pl.BlockSpec((None, 2 * D_MODEL, TN_OUT), lambda j, i: (l, 0, j)),
            pl.BlockSpec((TM_OUT, TN_OUT), lambda j, i: (i, j)),
        ],
        out_specs=pl.BlockSpec((TM_OUT, TN_OUT), lambda j, i: (i, j)),
        out_shape=jax.ShapeDtypeStruct((m, D_MODEL), f32),
        compiler_params=_params(2),
        name="outproj",
    )(ya, yb, w_out, x2d)


def _norm_kernel(x_ref, g_ref, o_ref):
    x = x_ref[...]
    ms = jnp.mean(x * x, axis=-1, keepdims=True)
    o_ref[...] = (x * lax.rsqrt(ms + EPS)) * g_ref[...]


def _final_norm(x2d, g):
    m = x2d.shape[0]
    return pl.pallas_call(
        _norm_kernel,
        grid=(m // TM_NORM,),
        in_specs=[pl.BlockSpec((TM_NORM, D_MODEL), lambda i: (i, 0)),
                  pl.BlockSpec((1, D_MODEL), lambda i: (0, 0))],
        out_specs=pl.BlockSpec((TM_NORM, D_MODEL), lambda i: (i, 0)),
        out_shape=jax.ShapeDtypeStruct((m, D_MODEL), f32),
        compiler_params=_params(1),
        name="final_norm",
    )(x2d, g)


def _pad_lanes(v):
    return jnp.pad(v.astype(f32), ((0, 0), (0, LANES - v.shape[1])))[:, None, :]


def kernel(x, norm_g, w_in, i_bias, f_bias, qk_conv, head_norm_g, lru_conv_w, lru_conv_b,
           w_a, b_a, w_x, b_x, lam, w_out, final_g):
    batch, seq, d = x.shape
    m = batch * seq
    nc = seq // LC
    nt = seq // T_LRU
    w_main, w_gate = _prep_weights(jnp.swapaxes(w_in, 1, 2))
    w_out_b = w_out.astype(bf16)
    w_a_b = w_a.astype(bf16)
    w_x_b = w_x.astype(bf16)
    ib = _pad_lanes(i_bias)
    fb = _pad_lanes(f_bias)
    row = lambda v: v[:, None, :]
    pm = jnp.asarray(_PM, dtype=bf16)
    pmt = jnp.asarray(_PM.T, dtype=bf16)

    x2d = x.reshape(m, d)
    for l in range(DEPTH):
        p, a, wi, e, wk, rt, dec = _inproj(l, x2d, row(norm_g), w_main, w_gate, ib, fb, seq // TM_IN)
        ya = _mlstm(l, p, a, wi, e, wk, rt, dec, pm, pmt, qk_conv, row(head_norm_g), batch, nc)
        yb = _lru(l, p, pm, pmt, lru_conv_w, row(lru_conv_b), w_a_b, w_x_b, row(b_a), row(b_x), row(lam),
                  batch, nt)
        x2d = _outproj(l, ya, yb, w_out_b, x2d)
    return _final_norm(x2d, final_g[None, :]).reshape(batch, seq, d)
```

```python
import numpy as np

import jax
import jax.numpy as jnp
from jax import lax
from jax.experimental import pallas as pl
from jax.experimental.pallas import tpu as pltpu

f32 = jnp.float32
bf16 = jnp.bfloat16

D_MODEL = 2048
DEPTH = 4
H_A = 8
DK_A = 128
DV_A = 256
H_B = 8
DB = 256
CONV_W = 4
LRU_C = 8.0
EPS = 1e-6

LANES = 128
SUBLANES = 8
N_MAIN = 6 * D_MODEL
N_IN = N_MAIN + 2 * H_A
C_GATE = 4 * D_MODEL
N_GROUPS = N_MAIN // LANES
G_Q, G_K, G_V, G_O, G_ZA, G_XB, G_ZB = 0, 8, 16, 32, 48, 64, 80
N_GROUPS_A = 64

N_GATE = 2 * H_A
TQ_PREP = 1024
Q_GATE = C_GATE // TQ_PREP
TM_IN = 1024
TN_IN = 1536
PB = 256
SEG = PB // SUBLANES
LC = PB
N_CHUNKS_IN = TM_IN // LC
T_LRU = 1024
HP_LRU = 4
TM_OUT = 512
TN_OUT = 1024
TM_NORM = 1024
HALO = (CONV_W - 1) * SUBLANES

NEG_BIG = -1e30
VMEM_LIMIT = 56 * 1024 * 1024


def _segment_major_matrix():
    p = np.arange(PB)
    pm = np.zeros((PB, PB), np.float32)
    pm[p, (p % SUBLANES) * SEG + p // SUBLANES] = 1.0
    return pm


_PM = _segment_major_matrix()


def _sigmoid(x):
    return 0.5 * jnp.tanh(0.5 * x) + 0.5


def _silu(x):
    hx = 0.5 * x
    return hx * jnp.tanh(hx) + hx


def _params(n_axes):
    return pltpu.CompilerParams(dimension_semantics=("arbitrary",) * n_axes, vmem_limit_bytes=VMEM_LIMIT)


def _prep_kernel(wt_ref, nxt_ref, wm_ref, wg_ref):
    q = pl.program_id(1)

    @pl.when(q < Q_GATE)
    def _():
        wm_ref[...] = wt_ref[...].T.astype(bf16)

    @pl.when(q >= Q_GATE)
    def _():
        wm_ref[...] = jnp.concatenate([wt_ref[N_GATE:TQ_PREP, :], nxt_ref[...]], axis=0).T.astype(bf16)

    @pl.when(q == Q_GATE)
    def _():
        zeros = jnp.zeros((LANES - H_A, D_MODEL), f32)
        wg_ref[...] = jnp.concatenate([wt_ref[0:H_A, :], zeros, wt_ref[H_A:N_GATE, :], zeros],
                                      axis=0).T.astype(bf16)


def _prep_weights(w_in_t):
    nxt_block = lambda l, q: (l, jnp.maximum(q + 1 - Q_GATE, 0) * (TQ_PREP // N_GATE) + C_GATE // N_GATE, 0)
    return pl.pallas_call(
        _prep_kernel,
        grid=(DEPTH, N_MAIN // TQ_PREP),
        in_specs=[pl.BlockSpec((None, TQ_PREP, D_MODEL), lambda l, q: (l, q, 0)),
                  pl.BlockSpec((None, N_GATE, D_MODEL), nxt_block)],
        out_specs=[pl.BlockSpec((None, D_MODEL, TQ_PREP), lambda l, q: (l, 0, q)),
                   pl.BlockSpec((None, D_MODEL, 2 * LANES), lambda l, q: (l, 0, 0))],
        out_shape=[jax.ShapeDtypeStruct((DEPTH, D_MODEL, N_MAIN), bf16),
                   jax.ShapeDtypeStruct((DEPTH, D_MODEL, 2 * LANES), bf16)],
        compiler_params=_params(2),
        name="prep_weights",
    )(w_in_t, w_in_t)


def _gate_chunk(gi, gf, ib, fb, m_st):
    ii = gi + ib
    lf = jax.nn.log_sigmoid(gf + fb)
    row = lax.broadcasted_iota(jnp.int32, (LC, LANES), 0)
    bc = lf
    d = 1
    while d < LC:
        bc = bc + jnp.where(row >= d, pltpu.roll(bc, d, axis=0), 0.0)
        d *= 2
    r = ii - bc
    cm = r
    d = 1
    while d < LC:
        cm = jnp.maximum(cm, jnp.where(row >= d, pltpu.roll(cm, d, axis=0), NEG_BIG))
        d *= 2
    mx = jnp.maximum(m_st, cm)
    mx_last = mx[LC - 1:LC, :]
    return (-mx, jnp.exp(m_st - mx), jnp.exp(-(bc + mx)), jnp.exp(r - mx_last),
            r.T[0:SUBLANES, :], jnp.exp(m_st - mx_last), bc[LC - 1:LC, :] + mx_last)


def _inproj_kernel(tiles_per_seq, x_ref, g_ref, w_ref, wg_ref, ib_ref, fb_ref,
                   p_ref, a_ref, wi_ref, e_ref, wk_ref, rt_ref, dec_ref, u_ref, gg_ref, m_ref):
    i = pl.program_id(0)
    j = pl.program_id(1)

    def matmul():
        acc = jnp.dot(u_ref[...], w_ref[...], preferred_element_type=f32)
        for g in range(TN_IN // LANES):
            p_ref[g] = acc[:, g * LANES:(g + 1) * LANES].astype(bf16)

    @pl.when(j == 0)
    def _():
        @pl.when(i % tiles_per_seq == 0)
        def _():
            m_ref[...] = jnp.zeros_like(m_ref)

        x = x_ref[...]
        ms = jnp.mean(x * x, axis=-1, keepdims=True)
        ub = ((x * lax.rsqrt(ms + EPS)) * g_ref[...]).astype(bf16)
        u_ref[...] = ub
        gg_ref[...] = jnp.dot(ub, wg_ref[...], preferred_element_type=f32)
        matmul()

    @pl.when(jnp.logical_and(j >= 1, j <= N_CHUNKS_IN))
    def _():
        c = j - 1
        rows = pl.ds(pl.multiple_of(c * LC, LC), LC)
        a, wi, e, wk, rt, dec, m_new = _gate_chunk(gg_ref[rows, :LANES], gg_ref[rows, LANES:],
                                                   ib_ref[...], fb_ref[...], m_ref[0:1, :])
        a_ref[rows, :] = a
        wi_ref[rows, :] = wi
        e_ref[rows, :] = e
        wk_ref[rows, :] = wk
        rt_ref[c] = rt
        dec_ref[c] = jnp.broadcast_to(dec, (SUBLANES, LANES))
        m_ref[...] = jnp.broadcast_to(m_new, (SUBLANES, LANES))
        matmul()

    @pl.when(j > N_CHUNKS_IN)
    def _():
        matmul()


def _inproj(l, x2d, norm_g, w_main, w_gate, ib, fb, tiles_per_seq):
    m = x2d.shape[0]
    grid = (m // TM_IN, N_MAIN // TN_IN)
    col = pl.BlockSpec((TM_IN, LANES), lambda i, j: (i, 0))
    vec = pl.BlockSpec((None, 1, LANES), lambda i, j: (l, 0, 0))
    kern = lambda *refs: _inproj_kernel(tiles_per_seq, *refs)
    return pl.pallas_call(
        kern,
        grid=grid,
        in_specs=[
            pl.BlockSpec((TM_IN, D_MODEL), lambda i, j: (i, 0)),
            pl.BlockSpec((None, 1, D_MODEL), lambda i, j: (l, 0, 0)),
            pl.BlockSpec((None, D_MODEL, TN_IN), lambda i, j: (l, 0, j)),
            pl.BlockSpec((None, D_MODEL, 2 * LANES), lambda i, j: (l, 0, 0)),
            vec, vec,
        ],
        out_specs=[
            pl.BlockSpec((TN_IN // LANES, TM_IN, LANES), lambda i, j: (j, i, 0)),
            col, col, col, col,
            pl.BlockSpec((N_CHUNKS_IN, SUBLANES, LC), lambda i, j: (i, 0, 0)),
            pl.BlockSpec((N_CHUNKS_IN, SUBLANES, LANES), lambda i, j: (i, 0, 0)),
        ],
        out_shape=[jax.ShapeDtypeStruct((N_GROUPS, m, LANES), bf16)]
        + [jax.ShapeDtypeStruct((m, LANES), f32)] * 4
        + [jax.ShapeDtypeStruct((m // LC, SUBLANES, LC), f32),
           jax.ShapeDtypeStruct((m // LC, SUBLANES, LANES), f32)],
        scratch_shapes=[pltpu.VMEM((TM_IN, D_MODEL), bf16),
                        pltpu.VMEM((TM_IN, 2 * LANES), f32),
                        pltpu.VMEM((SUBLANES, LANES), f32)],
        compiler_params=_params(2),
        name="inproj",
    )(x2d, norm_g, w_main, w_gate, ib, fb)


def _causal_conv_block(cur, prev_tail, w, row8):
    tail = cur[PB - HALO:PB]
    before = [pltpu.roll(jnp.where(row8 == SUBLANES - 1,
                                   prev_tail[i * SUBLANES:(i + 1) * SUBLANES],
                                   tail[i * SUBLANES:(i + 1) * SUBLANES]), 1, axis=0)
              for i in range(CONV_W - 1)]
    ext = jnp.concatenate(before + [cur], axis=0)
    acc = cur * w[CONV_W - 1:CONV_W, :]
    for j in range(1, CONV_W):
        lo = HALO - j * SUBLANES
        acc = acc + ext[lo:lo + PB] * w[CONV_W - 1 - j:CONV_W - j, :]
    return acc, tail


def _mlstm_kernel(p_ref, a_ref, wi_ref, e_ref, wk_ref, rt_ref, dec_ref, pm_ref, pmt_ref, cw_ref, hng_ref,
                  ya_ref, c_ref, n_ref, halo_ref):
    @pl.when(pl.program_id(1) == 0)
    def _():
        c_ref[...] = jnp.zeros_like(c_ref)
        n_ref[...] = jnp.zeros_like(n_ref)
        halo_ref[...] = jnp.zeros_like(halo_ref)

    row8 = lax.broadcasted_iota(jnp.int32, (SUBLANES, 2 * DK_A), 0)
    causal = (lax.broadcasted_iota(jnp.int32, (LC, LC), 0)
              >= lax.broadcasted_iota(jnp.int32, (LC, LC), 1))
    lane = lax.broadcasted_iota(jnp.int32, (1, 2 * DK_A), 1)
    qk_scale = jnp.where(lane < DK_A, 1.0, DK_A ** -0.5)
    pm = pm_ref[...]
    pmt = pmt_ref[...]

    def wide(g0):
        return jnp.concatenate([p_ref[g0], p_ref[g0 + 1]], axis=-1)

    heads = range(H_A)
    qk_seg = [jnp.dot(pm, jnp.concatenate([p_ref[G_Q + h], p_ref[G_K + h]], axis=-1),
                      preferred_element_type=f32) for h in heads]
    acts = []
    for h in heads:
        w_qk = jnp.concatenate([cw_ref[:, (G_Q + h) * LANES:(G_Q + h + 1) * LANES],
                                cw_ref[:, (G_K + h) * LANES:(G_K + h + 1) * LANES]], axis=-1)
        y, tail = _causal_conv_block(qk_seg[h], halo_ref[h], w_qk, row8)
        halo_ref[h] = tail
        acts.append((_silu(y) * qk_scale).astype(bf16))
    qk = [jnp.dot(pmt, acts[h], preferred_element_type=f32) for h in heads]
    qb = [qk[h][:, :DK_A].astype(bf16) for h in heads]
    kb = [qk[h][:, DK_A:].astype(bf16) for h in heads]
    v = [wide(G_V + 2 * h) for h in heads]

    s_raw = [lax.dot_general(qb[h], kb[h], (((1,), (1,)), ((), ())), preferred_element_type=f32)
             for h in heads]
    q_c = [jnp.dot(qb[h], c_ref[h].astype(bf16), preferred_element_type=f32) for h in heads]
    s = []
    for h in heads:
        dmat = jnp.exp(jnp.where(causal, a_ref[:, h:h + 1] + rt_ref[h:h + 1, :], NEG_BIG))
        s.append(s_raw[h] * dmat)
    s_v = [jnp.dot(s[h].astype(bf16), v[h], preferred_element_type=f32) for h in heads]

    kw = [qk[h][:, DK_A:] * wk_ref[:, h:h + 1] for h in heads]
    kw_v = [jnp.dot(kw[h].T.astype(bf16), v[h], preferred_element_type=f32) for h in heads]
    hc = []
    for h in heads:
        wi = wi_ref[:, h:h + 1]
        n_st = n_ref[h]
        dec = dec_ref[0:1, h:h + 1]
        den = (jnp.sum(s[h], axis=-1, keepdims=True)
               + wi * jnp.sum(qk[h][:, :DK_A] * n_st, axis=-1, keepdims=True))
        hc.append((s_v[h] + wi * q_c[h]) * (1.0 / jnp.maximum(jnp.abs(den), e_ref[:, h:h + 1])))
        c_ref[h] = dec * c_ref[h] + kw_v[h]
        n_ref[h] = dec * n_st + jnp.sum(kw[h], axis=0, keepdims=True)

    for h in heads:
        hg = hc[h] * _sigmoid(wide(G_O + 2 * h).astype(f32))
        mu = jnp.mean(hg, axis=-1, keepdims=True)
        dev = hg - mu
        var = jnp.mean(dev * dev, axis=-1, keepdims=True)
        hn = (dev * lax.rsqrt(var + EPS)) * hng_ref[:, h * DV_A:(h + 1) * DV_A]
        ya_ref[h] = (hn * _silu(wide(G_ZA + 2 * h).astype(f32))).astype(bf16)


def _mlstm(l, p, a, wi, e, wk, rt, dec, pm, pmt, conv_w, hn_g, batch, nc):
    m = p.shape[1]
    blk = pl.BlockSpec((LC, LANES), lambda b, c: (b * nc + c, 0))
    mat = pl.BlockSpec((PB, PB), lambda b, c: (0, 0))
    return pl.pallas_call(
        _mlstm_kernel,
        grid=(batch, nc),
        in_specs=[
            pl.BlockSpec((N_GROUPS_A, LC, LANES), lambda b, c: (0, b * nc + c, 0)),
            blk, blk, blk, blk,
            pl.BlockSpec((None, SUBLANES, LC), lambda b, c: (b * nc + c, 0, 0)),
            pl.BlockSpec((None, SUBLANES, LANES), lambda b, c: (b * nc + c, 0, 0)),
            mat, mat,
            pl.BlockSpec((None, CONV_W, 2 * H_A * DK_A), lambda b, c: (l, 0, 0)),
            pl.BlockSpec((None, 1, H_A * DV_A), lambda b, c: (l, 0, 0)),
        ],
        out_specs=pl.BlockSpec((H_A, LC, DV_A), lambda b, c: (0, b * nc + c, 0)),
        out_shape=jax.ShapeDtypeStruct((H_A, m, DV_A), bf16),
        scratch_shapes=[
            pltpu.VMEM((H_A, DK_A, DV_A), f32),
            pltpu.VMEM((H_A, 1, DK_A), f32),
            pltpu.VMEM((H_A, HALO, 2 * DK_A), f32),
        ],
        compiler_params=_params(2),
        name="mlstm",
    )(p, a, wi, e, wk, rt, dec, pm, pmt, conv_w, hn_g)


def _lru_kernel(xb_ref, zb_ref, pm_ref, pmt_ref, cw_ref, cb_ref, wa_ref, wx_ref, ba_ref, bx_ref, lam_ref,
                yb_ref, halo_ref, carry_ref):
    @pl.when(pl.program_id(2) == 0)
    def _():
        halo_ref[...] = jnp.zeros_like(halo_ref)
        carry_ref[...] = jnp.zeros_like(carry_ref)

    n_blocks = T_LRU // PB
    heads = range(HP_LRU)
    row8 = lax.broadcasted_iota(jnp.int32, (SUBLANES, DB), 0)
    pm = pm_ref[...]
    pmt = pmt_ref[...]

    def wide(ref, k):
        return jnp.concatenate([ref[2 * k], ref[2 * k + 1]], axis=-1)

    def chan(ref, k):
        return ref[:, k * DB:(k + 1) * DB]

    x_seg = [[jnp.dot(pm, wide(xb_ref, k)[b * PB:(b + 1) * PB], preferred_element_type=f32)
              for b in range(n_blocks)] for k in heads]
    z_seg = [[jnp.dot(pm, wide(zb_ref, k)[b * PB:(b + 1) * PB], preferred_element_type=f32)
              for b in range(n_blocks)] for k in heads]

    xc = []
    for k in heads:
        prev_tail = halo_ref[k]
        blocks = []
        for b in range(n_blocks):
            y, prev_tail = _causal_conv_block(x_seg[k][b], prev_tail, chan(cw_ref, k), row8)
            blocks.append(y + chan(cb_ref, k))
        halo_ref[k] = prev_tail
        xc.append(jnp.concatenate(blocks, axis=0))

    xcb = [xc[k].astype(bf16) for k in heads]
    pre_r = [jnp.dot(xcb[k], wa_ref[k], preferred_element_type=f32) + chan(ba_ref, k) for k in heads]
    pre_i = [jnp.dot(xcb[k], wx_ref[k], preferred_element_type=f32) + chan(bx_ref, k) for k in heads]
    a, u = [], []
    for k in heads:
        lam = chan(lam_ref, k)
        softplus_neg = jnp.maximum(-lam, 0.0) + jnp.log1p(jnp.exp(-jnp.abs(lam)))
        half_rate = (-0.5 * LRU_C) * softplus_neg
        log_a = half_rate * jnp.tanh(0.5 * pre_r[k]) + half_rate
        ak = jnp.exp(log_a)
        a.append(ak)
        var_in = jnp.tanh(-log_a) * (ak * ak + 1.0)
        scale_in = jnp.where(var_in > 0.0, var_in * lax.rsqrt(var_in), 0.0)
        u.append((xc[k] * _sigmoid(pre_i[k])) * scale_in)

    carry = [carry_ref[k] for k in heads]
    for b in range(n_blocks):
        lo = b * PB
        yp = []
        for k in heads:
            gate = _silu(z_seg[k][b])
            h = u[k][lo:lo + SUBLANES]
            p = a[k][lo:lo + SUBLANES]
            hs, ps = [h], [p]
            for j in range(1, SEG):
                aj = a[k][lo + j * SUBLANES:lo + (j + 1) * SUBLANES]
                h = aj * h + u[k][lo + j * SUBLANES:lo + (j + 1) * SUBLANES]
                p = aj * p
                hs.append(h)
                ps.append(p)
            for d in (1, 2, 4):
                keep = row8 >= d
                h_sh = jnp.where(keep, pltpu.roll(h, d, axis=0), 0.0)
                p_sh = jnp.where(keep, pltpu.roll(p, d, axis=0), 1.0)
                h = h + p * h_sh
                p = p * p_sh
            seg_end = h + p * carry[k]
            seg_in = jnp.where(row8 == 0, carry[k], pltpu.roll(seg_end, 1, axis=0))
            carry[k] = jnp.broadcast_to(seg_end[SUBLANES - 1:SUBLANES, :], (SUBLANES, DB))
            yp.append(jnp.concatenate(
                [(hs[j] + ps[j] * seg_in) * gate[j * SUBLANES:(j + 1) * SUBLANES] for j in range(SEG)],
                axis=0).astype(bf16))
        for k in heads:
            yb_ref[k, lo:lo + PB, :] = jnp.dot(pmt, yp[k], preferred_element_type=f32).astype(bf16)
    for k in heads:
        carry_ref[k] = carry[k]


def _lru(l, p, pm, pmt, conv_w, conv_b, w_a, w_x, b_a, b_x, lam, batch, nt):
    m = p.shape[1]
    width = HP_LRU * DB
    groups = width // LANES
    vec = pl.BlockSpec((None, 1, width), lambda b, h, t: (l, 0, h))
    mat = pl.BlockSpec((PB, PB), lambda b, h, t: (0, 0))
    gate_w = pl.BlockSpec((None, HP_LRU, DB, DB), lambda b, h, t: (l, h, 0, 0))
    return pl.pallas_call(
        _lru_kernel,
        grid=(batch, H_B // HP_LRU, nt),
        in_specs=[
            pl.BlockSpec((groups, T_LRU, LANES), lambda b, h, t: (G_XB // groups + h, b * nt + t, 0)),
            pl.BlockSpec((groups, T_LRU, LANES), lambda b, h, t: (G_ZB // groups + h, b * nt + t, 0)),
            mat, mat,
            pl.BlockSpec((None, CONV_W, width), lambda b, h, t: (l, 0, h)),
            vec, gate_w, gate_w, vec, vec, vec,
        ],
        out_specs=pl.BlockSpec((HP_LRU, T_LRU, DB), lambda b, h, t: (h, b * nt + t, 0)),
        out_shape=jax.ShapeDtypeStruct((H_B, m, DB), bf16),
        scratch_shapes=[pltpu.VMEM((HP_LRU, HALO, DB), f32), pltpu.VMEM((HP_LRU, SUBLANES, DB), f32)],
        compiler_params=_params(3),
        name="rglru",
    )(p, p, pm, pmt, conv_w, conv_b, w_a, w_x, b_a, b_x, lam)


def _outproj_kernel(ya_ref, yb_ref, w_ref, x_ref, o_ref):
    y = jnp.concatenate([ya_ref[h] for h in range(H_A)] + [yb_ref[h] for h in range(H_B)], axis=-1)
    o_ref[...] = x_ref[...] + jnp.dot(y, w_ref[...], preferred_element_type=f32)


def _outproj(l, ya, yb, w_out, x2d):
    m = x2d.shape[0]
    return pl.pallas_call(
        _outproj_kernel,
        grid=(D_MODEL // TN_OUT, m // TM_OUT),
        in_specs=[
            pl.BlockSpec((H_A, TM_OUT, DV_A), lambda j, i: (0, i, 0)),
            pl.BlockSpec((H_B, TM_OUT, DB), lambda j, i: (0, i, 0)),
            pl.BlockSpec((None, 2 * D_MODEL, TN_OUT), lambda j, i: (l, 0, j)),
            pl.BlockSpec((TM_OUT, TN_OUT), lambda j, i: (i, j)),
        ],
        out_specs=pl.BlockSpec((TM_OUT, TN_OUT), lambda j, i: (i, j)),
        out_shape=jax.ShapeDtypeStruct((m, D_MODEL), f32),
        compiler_params=_params(2),
        name="outproj",
    )(ya, yb, w_out, x2d)


def _norm_kernel(x_ref, g_ref, o_ref):
    x = x_ref[...]
    ms = jnp.mean(x * x, axis=-1, keepdims=True)
    o_ref[...] = (x * lax.rsqrt(ms + EPS)) * g_ref[...]


def _final_norm(x2d, g):
    m = x2d.shape[0]
    return pl.pallas_call(
        _norm_kernel,
        grid=(m // TM_NORM,),
        in_specs=[pl.BlockSpec((TM_NORM, D_MODEL), lambda i: (i, 0)),
                  pl.BlockSpec((1, D_MODEL), lambda i: (0, 0))],
        out_specs=pl.BlockSpec((TM_NORM, D_MODEL), lambda i: (i, 0)),
        out_shape=jax.ShapeDtypeStruct((m, D_MODEL), f32),
        compiler_params=_params(1),
        name="final_norm",
    )(x2d, g)


def _pad_lanes(v):
    return jnp.pad(v.astype(f32), ((0, 0), (0, LANES - v.shape[1])))[:, None, :]


def kernel(x, norm_g, w_in, i_bias, f_bias, qk_conv, head_norm_g, lru_conv_w, lru_conv_b,
           w_a, b_a, w_x, b_x, lam, w_out, final_g):
    batch, seq, d = x.shape
    m = batch * seq
    nc = seq // LC
    nt = seq // T_LRU
    w_main, w_gate = _prep_weights(jnp.swapaxes(w_in, 1, 2))
    w_out_b = w_out.astype(bf16)
    w_a_b = w_a.astype(bf16)
    w_x_b = w_x.astype(bf16)
    ib = _pad_lanes(i_bias)
    fb = _pad_lanes(f_bias)
    row = lambda v: v[:, None, :]
    pm = jnp.asarray(_PM, dtype=bf16)
    pmt = jnp.asarray(_PM.T, dtype=bf16)

    x2d = x.reshape(m, d)
    for l in range(DEPTH):
        p, a, wi, e, wk, rt, dec = _inproj(l, x2d, row(norm_g), w_main, w_gate, ib, fb, seq // TM_IN)
        ya = _mlstm(l, p, a, wi, e, wk, rt, dec, pm, pmt, qk_conv, row(head_norm_g), batch, nc)
        yb = _lru(l, p, pm, pmt, lru_conv_w, row(lru_conv_b), w_a_b, w_x_b, row(b_a), row(b_x), row(lam),
                  batch, nt)
        x2d = _outproj(l, ya, yb, w_out_b, x2d)
    return _final_norm(x2d, final_g[None, :]).reshape(batch, seq, d)
```

```python
import numpy as np

import jax
import jax.numpy as jnp
from jax import lax
from jax.experimental import pallas as pl
from jax.experimental.pallas import tpu as pltpu

f32 = jnp.float32
bf16 = jnp.bfloat16

D_MODEL = 2048
DEPTH = 4
H_A = 8
DK_A = 128
DV_A = 256
H_B = 8
DB = 256
CONV_W = 4
LRU_C = 8.0
EPS = 1e-6

LANES = 128
SUBLANES = 8
N_MAIN = 6 * D_MODEL
N_IN = N_MAIN + 2 * H_A
C_GATE = 4 * D_MODEL
C_O = 2 * D_MODEL
C_ZB_MAIN = 5 * D_MODEL
N_GROUPS = N_MAIN // LANES
G_Q, G_K, G_V, G_O, G_ZA, G_XB, G_ZB = 0, 8, 16, 32, 48, 64, 80
N_GROUPS_A = 64

N_GATE = 2 * H_A
TQ_PREP = 1024
Q_GATE = C_GATE // TQ_PREP
TM_IN = 1024
TN_IN = 1536
PB = 256
SEG = PB // SUBLANES
LC = PB
N_CHUNKS_IN = TM_IN // LC
T_LRU = 1024
HP_LRU = 4
TM_OUT = 512
TN_OUT = 1024
TM_NORM = 1024
HALO = (CONV_W - 1) * SUBLANES

NEG_BIG = -1e30
VMEM_LIMIT = 56 * 1024 * 1024


def _segment_major_matrix():
    p = np.arange(PB)
    pm = np.zeros((PB, PB), np.float32)
    pm[p, (p % SUBLANES) * SEG + p // SUBLANES] = 1.0
    return pm


_PM = _segment_major_matrix()


def _sigmoid_of_half(hx):
    return 0.5 * jnp.tanh(hx) + 0.5


def _silu_of_half(hx):
    return hx * jnp.tanh(hx) + hx


def _silu(x):
    return _silu_of_half(0.5 * x)


def _params(n_axes):
    return pltpu.CompilerParams(dimension_semantics=("arbitrary",) * n_axes, vmem_limit_bytes=VMEM_LIMIT)


def _prep_kernel(wt_ref, nxt_ref, wm_ref, wg_ref):
    q = pl.program_id(1)
    col0 = q * TQ_PREP
    halved = jnp.logical_or(jnp.logical_and(col0 >= C_O, col0 < C_GATE), col0 >= C_ZB_MAIN)
    scale = jnp.where(halved, 0.5, 1.0)

    @pl.when(q < Q_GATE)
    def _():
        wm_ref[...] = (wt_ref[...] * scale).T.astype(bf16)

    @pl.when(q >= Q_GATE)
    def _():
        rows = jnp.concatenate([wt_ref[N_GATE:TQ_PREP, :], nxt_ref[...]], axis=0)
        wm_ref[...] = (rows * scale).T.astype(bf16)

    @pl.when(q == Q_GATE)
    def _():
        zeros = jnp.zeros((LANES - H_A, D_MODEL), f32)
        wg_ref[...] = jnp.concatenate([wt_ref[0:H_A, :], zeros, wt_ref[H_A:N_GATE, :], zeros],
                                      axis=0).T.astype(bf16)


def _prep_weights(w_in_t):
    nxt_block = lambda l, q: (l, jnp.maximum(q + 1 - Q_GATE, 0) * (TQ_PREP // N_GATE) + C_GATE // N_GATE, 0)
    return pl.pallas_call(
        _prep_kernel,
        grid=(DEPTH, N_MAIN // TQ_PREP),
        in_specs=[pl.BlockSpec((None, TQ_PREP, D_MODEL), lambda l, q: (l, q, 0)),
                  pl.BlockSpec((None, N_GATE, D_MODEL), nxt_block)],
        out_specs=[pl.BlockSpec((None, D_MODEL, TQ_PREP), lambda l, q: (l, 0, q)),
                   pl.BlockSpec((None, D_MODEL, 2 * LANES), lambda l, q: (l, 0, 0))],
        out_shape=[jax.ShapeDtypeStruct((DEPTH, D_MODEL, N_MAIN), bf16),
                   jax.ShapeDtypeStruct((DEPTH, D_MODEL, 2 * LANES), bf16)],
        compiler_params=_params(2),
        name="prep_weights",
    )(w_in_t, w_in_t)


def _gate_chunk(gi, gf, ib, fb, m_st):
    ii = gi + ib
    lf = jax.nn.log_sigmoid(gf + fb)
    row = lax.broadcasted_iota(jnp.int32, (LC, LANES), 0)
    bc = lf
    d = 1
    while d < LC:
        bc = bc + jnp.where(row >= d, pltpu.roll(bc, d, axis=0), 0.0)
        d *= 2
    r = ii - bc
    cm = r
    d = 1
    while d < LC:
        cm = jnp.maximum(cm, jnp.where(row >= d, pltpu.roll(cm, d, axis=0), NEG_BIG))
        d *= 2
    mx = jnp.maximum(m_st, cm)
    mx_last = mx[LC - 1:LC, :]
    return (-mx, jnp.exp(m_st - mx), jnp.exp(-(bc + mx)), jnp.exp(r - mx_last),
            r.T[0:SUBLANES, :], jnp.exp(m_st - mx_last), bc[LC - 1:LC, :] + mx_last)


def _inproj_kernel(tiles_per_seq, x_ref, g_ref, w_ref, wg_ref, ib_ref, fb_ref,
                   p_ref, a_ref, wi_ref, e_ref, wk_ref, rt_ref, dec_ref, u_ref, gg_ref, m_ref):
    i = pl.program_id(0)
    j = pl.program_id(1)

    def matmul():
        acc = jnp.dot(u_ref[...], w_ref[...], preferred_element_type=f32)
        for g in range(TN_IN // LANES):
            p_ref[g] = acc[:, g * LANES:(g + 1) * LANES].astype(bf16)

    @pl.when(j == 0)
    def _():
        @pl.when(i % tiles_per_seq == 0)
        def _():
            m_ref[...] = jnp.zeros_like(m_ref)

        x = x_ref[...]
        ms = jnp.mean(x * x, axis=-1, keepdims=True)
        ub = ((x * lax.rsqrt(ms + EPS)) * g_ref[...]).astype(bf16)
        u_ref[...] = ub
        gg_ref[...] = jnp.dot(ub, wg_ref[...], preferred_element_type=f32)
        matmul()

    @pl.when(jnp.logical_and(j >= 1, j <= N_CHUNKS_IN))
    def _():
        c = j - 1
        rows = pl.ds(pl.multiple_of(c * LC, LC), LC)
        a, wi, e, wk, rt, dec, m_new = _gate_chunk(gg_ref[rows, :LANES], gg_ref[rows, LANES:],
                                                   ib_ref[...], fb_ref[...], m_ref[0:1, :])
        a_ref[rows, :] = a
        wi_ref[rows, :] = wi
        e_ref[rows, :] = e
        wk_ref[rows, :] = wk
        rt_ref[c] = rt
        dec_ref[c] = jnp.broadcast_to(dec, (SUBLANES, LANES))
        m_ref[...] = jnp.broadcast_to(m_new, (SUBLANES, LANES))
        matmul()

    @pl.when(j > N_CHUNKS_IN)
    def _():
        matmul()


def _inproj(l, x2d, norm_g, w_main, w_gate, ib, fb, tiles_per_seq):
    m = x2d.shape[0]
    grid = (m // TM_IN, N_MAIN // TN_IN)
    col = pl.BlockSpec((TM_IN, LANES), lambda i, j: (i, 0))
    vec = pl.BlockSpec((None, 1, LANES), lambda i, j: (l, 0, 0))
    kern = lambda *refs: _inproj_kernel(tiles_per_seq, *refs)
    return pl.pallas_call(
        kern,
        grid=grid,
        in_specs=[
            pl.BlockSpec((TM_IN, D_MODEL), lambda i, j: (i, 0)),
            pl.BlockSpec((None, 1, D_MODEL), lambda i, j: (l, 0, 0)),
            pl.BlockSpec((None, D_MODEL, TN_IN), lambda i, j: (l, 0, j)),
            pl.BlockSpec((None, D_MODEL, 2 * LANES), lambda i, j: (l, 0, 0)),
            vec, vec,
        ],
        out_specs=[
            pl.BlockSpec((TN_IN // LANES, TM_IN, LANES), lambda i, j: (j, i, 0)),
            col, col, col, col,
            pl.BlockSpec((N_CHUNKS_IN, SUBLANES, LC), lambda i, j: (i, 0, 0)),
            pl.BlockSpec((N_CHUNKS_IN, SUBLANES, LANES), lambda i, j: (i, 0, 0)),
        ],
        out_shape=[jax.ShapeDtypeStruct((N_GROUPS, m, LANES), bf16)]
        + [jax.ShapeDtypeStruct((m, LANES), f32)] * 4
        + [jax.ShapeDtypeStruct((m // LC, SUBLANES, LC), f32),
           jax.ShapeDtypeStruct((m // LC, SUBLANES, LANES), f32)],
        scratch_shapes=[pltpu.VMEM((TM_IN, D_MODEL), bf16),
                        pltpu.VMEM((TM_IN, 2 * LANES), f32),
                        pltpu.VMEM((SUBLANES, LANES), f32)],
        compiler_params=_params(2),
        name="inproj",
    )(x2d, norm_g, w_main, w_gate, ib, fb)


def _causal_conv_block(cur, prev_tail, w, row8):
    tail = cur[PB - HALO:PB]
    before = [pltpu.roll(jnp.where(row8 == SUBLANES - 1,
                                   prev_tail[i * SUBLANES:(i + 1) * SUBLANES],
                                   tail[i * SUBLANES:(i + 1) * SUBLANES]), 1, axis=0)
              for i in range(CONV_W - 1)]
    ext = jnp.concatenate(before + [cur], axis=0)
    acc = cur * w[CONV_W - 1:CONV_W, :]
    for j in range(1, CONV_W):
        lo = HALO - j * SUBLANES
        acc = acc + ext[lo:lo + PB] * w[CONV_W - 1 - j:CONV_W - j, :]
    return acc, tail


def _mlstm_kernel(p_ref, a_ref, wi_ref, e_ref, wk_ref, rt_ref, dec_ref, pm_ref, pmt_ref, cw_ref, hng_ref,
                  ya_ref, c_ref, n_ref, halo_ref):
    @pl.when(pl.program_id(1) == 0)
    def _():
        c_ref[...] = jnp.zeros_like(c_ref)
        n_ref[...] = jnp.zeros_like(n_ref)
        halo_ref[...] = jnp.zeros_like(halo_ref)

    row8 = lax.broadcasted_iota(jnp.int32, (SUBLANES, 2 * DK_A), 0)
    causal = (lax.broadcasted_iota(jnp.int32, (LC, LC), 0)
              >= lax.broadcasted_iota(jnp.int32, (LC, LC), 1))
    lane = lax.broadcasted_iota(jnp.int32, (1, 2 * DK_A), 1)
    qk_scale = jnp.where(lane < DK_A, 1.0, DK_A ** -0.5)
    pm = pm_ref[...]
    pmt = pmt_ref[...]

    def wide(g0):
        return jnp.concatenate([p_ref[g0], p_ref[g0 + 1]], axis=-1)

    heads = range(H_A)
    qk_seg = [jnp.dot(pm, jnp.concatenate([p_ref[G_Q + h], p_ref[G_K + h]], axis=-1),
                      preferred_element_type=f32) for h in heads]
    acts = []
    for h in heads:
        w_qk = jnp.concatenate([cw_ref[:, (G_Q + h) * LANES:(G_Q + h + 1) * LANES],
                                cw_ref[:, (G_K + h) * LANES:(G_K + h + 1) * LANES]], axis=-1)
        y, tail = _causal_conv_block(qk_seg[h], halo_ref[h], w_qk, row8)
        halo_ref[h] = tail
        acts.append((_silu(y) * qk_scale).astype(bf16))
    qk = [jnp.dot(pmt, acts[h], preferred_element_type=f32) for h in heads]
    qb = [qk[h][:, :DK_A].astype(bf16) for h in heads]
    kb = [qk[h][:, DK_A:].astype(bf16) for h in heads]
    v = [wide(G_V + 2 * h) for h in heads]

    s_raw = [lax.dot_general(qb[h], kb[h], (((1,), (1,)), ((), ())), preferred_element_type=f32)
             for h in heads]
    q_c = [jnp.dot(qb[h], c_ref[h].astype(bf16), preferred_element_type=f32) for h in heads]
    s = []
    for h in heads:
        dmat = jnp.exp(jnp.where(causal, a_ref[:, h:h + 1] + rt_ref[h:h + 1, :], NEG_BIG))
        s.append(s_raw[h] * dmat)
    s_v = [jnp.dot(s[h].astype(bf16), v[h], preferred_element_type=f32) for h in heads]

    kw = [qk[h][:, DK_A:] * wk_ref[:, h:h + 1] for h in heads]
    kw_v = [jnp.dot(kw[h].T.astype(bf16), v[h], preferred_element_type=f32) for h in heads]
    hc = []
    for h in heads:
        wi = wi_ref[:, h:h + 1]
        n_st = n_ref[h]
        dec = dec_ref[0:1, h:h + 1]
        den = (jnp.sum(s[h], axis=-1, keepdims=True)
               + wi * jnp.sum(qk[h][:, :DK_A] * n_st, axis=-1, keepdims=True))
        hc.append((s_v[h] + wi * q_c[h]) * (1.0 / jnp.maximum(jnp.abs(den), e_ref[:, h:h + 1])))
        c_ref[h] = dec * c_ref[h] + kw_v[h]
        n_ref[h] = dec * n_st + jnp.sum(kw[h], axis=0, keepdims=True)

    for h in heads:
        hg = hc[h] * _sigmoid_of_half(wide(G_O + 2 * h).astype(f32))
        mu = jnp.mean(hg, axis=-1, keepdims=True)
        dev = hg - mu
        var = jnp.mean(dev * dev, axis=-1, keepdims=True)
        hn = (dev * lax.rsqrt(var + EPS)) * hng_ref[:, h * DV_A:(h + 1) * DV_A]
        ya_ref[h] = (hn * _silu_of_half(wide(G_ZA + 2 * h).astype(f32))).astype(bf16)


def _mlstm(l, p, a, wi, e, wk, rt, dec, pm, pmt, conv_w, hn_g, batch, nc):
    m = p.shape[1]
    blk = pl.BlockSpec((LC, LANES), lambda b, c: (b * nc + c, 0))
    mat = pl.BlockSpec((PB, PB), lambda b, c: (0, 0))
    return pl.pallas_call(
        _mlstm_kernel,
        grid=(batch, nc),
        in_specs=[
            pl.BlockSpec((N_GROUPS_A, LC, LANES), lambda b, c: (0, b * nc + c, 0)),
            blk, blk, blk, blk,
            pl.BlockSpec((None, SUBLANES, LC), lambda b, c: (b * nc + c, 0, 0)),
            pl.BlockSpec((None, SUBLANES, LANES), lambda b, c: (b * nc + c, 0, 0)),
            mat, mat,
            pl.BlockSpec((None, CONV_W, 2 * H_A * DK_A), lambda b, c: (l, 0, 0)),
            pl.BlockSpec((None, 1, H_A * DV_A), lambda b, c: (l, 0, 0)),
        ],
        out_specs=pl.BlockSpec((H_A, LC, DV_A), lambda b, c: (0, b * nc + c, 0)),
        out_shape=jax.ShapeDtypeStruct((H_A, m, DV_A), bf16),
        scratch_shapes=[
            pltpu.VMEM((H_A, DK_A, DV_A), f32),
            pltpu.VMEM((H_A, 1, DK_A), f32),
            pltpu.VMEM((H_A, HALO, 2 * DK_A), f32),
        ],
        compiler_params=_params(2),
        name="mlstm",
    )(p, a, wi, e, wk, rt, dec, pm, pmt, conv_w, hn_g)


def _lru_kernel(xb_ref, zb_ref, pm_ref, pmt_ref, cw_ref, cb_ref, wa_ref, wx_ref, ba_ref, bx_ref, lam_ref,
                yb_ref, halo_ref, carry_ref):
    @pl.when(pl.program_id(2) == 0)
    def _():
        halo_ref[...] = jnp.zeros_like(halo_ref)
        carry_ref[...] = jnp.zeros_like(carry_ref)

    n_blocks = T_LRU // PB
    heads = range(HP_LRU)
    row8 = lax.broadcasted_iota(jnp.int32, (SUBLANES, DB), 0)
    pm = pm_ref[...]
    pmt = pmt_ref[...]

    def wide(ref, k):
        return jnp.concatenate([ref[2 * k], ref[2 * k + 1]], axis=-1)

    def chan(ref, k):
        return ref[:, k * DB:(k + 1) * DB]

    x_seg = [[jnp.dot(pm, wide(xb_ref, k)[b * PB:(b + 1) * PB], preferred_element_type=f32)
              for b in range(n_blocks)] for k in heads]
    z_seg = [[jnp.dot(pm, wide(zb_ref, k)[b * PB:(b + 1) * PB], preferred_element_type=f32)
              for b in range(n_blocks)] for k in heads]

    xc = []
    for k in heads:
        prev_tail = halo_ref[k]
        blocks = []
        for b in range(n_blocks):
            y, prev_tail = _causal_conv_block(x_seg[k][b], prev_tail, chan(cw_ref, k), row8)
            blocks.append(y + chan(cb_ref, k))
        halo_ref[k] = prev_tail
        xc.append(jnp.concatenate(blocks, axis=0))

    xcb = [xc[k].astype(bf16) for k in heads]
    pre_r = [jnp.dot(xcb[k], wa_ref[k], preferred_element_type=f32) + chan(ba_ref, k) for k in heads]
    pre_i = [jnp.dot(xcb[k], wx_ref[k], preferred_element_type=f32) + chan(bx_ref, k) for k in heads]
    a, u = [], []
    for k in heads:
        lam = chan(lam_ref, k)
        softplus_neg = jnp.maximum(-lam, 0.0) + jnp.log1p(jnp.exp(-jnp.abs(lam)))
        half_rate = (-0.5 * LRU_C) * softplus_neg
        log_a = half_rate * jnp.tanh(pre_r[k]) + half_rate
        ak = jnp.exp(log_a)
        a.append(ak)
        var_in = jnp.tanh(-log_a) * (ak * ak + 1.0)
        scale_in = jnp.where(var_in > 0.0, var_in * lax.rsqrt(var_in), 0.0)
        u.append((xc[k] * _sigmoid_of_half(pre_i[k])) * scale_in)

    carry = [carry_ref[k] for k in heads]
    for b in range(n_blocks):
        lo = b * PB
        yp = []
        for k in heads:
            gate = _silu_of_half(z_seg[k][b])
            h = u[k][lo:lo + SUBLANES]
            p = a[k][lo:lo + SUBLANES]
            hs, ps = [h], [p]
            for j in range(1, SEG):
                aj = a[k][lo + j * SUBLANES:lo + (j + 1) * SUBLANES]
                h = aj * h + u[k][lo + j * SUBLANES:lo + (j + 1) * SUBLANES]
                p = aj * p
                hs.append(h)
                ps.append(p)
            for d in (1, 2, 4):
                keep = row8 >= d
                h_sh = jnp.where(keep, pltpu.roll(h, d, axis=0), 0.0)
                p_sh = jnp.where(keep, pltpu.roll(p, d, axis=0), 1.0)
                h = h + p * h_sh
                p = p * p_sh
            seg_end = h + p * carry[k]
            seg_in = jnp.where(row8 == 0, carry[k], pltpu.roll(seg_end, 1, axis=0))
            carry[k] = jnp.broadcast_to(seg_end[SUBLANES - 1:SUBLANES, :], (SUBLANES, DB))
            yp.append(jnp.concatenate(
                [(hs[j] + ps[j] * seg_in) * gate[j * SUBLANES:(j + 1) * SUBLANES] for j in range(SEG)],
                axis=0).astype(bf16))
        for k in heads:
            yb_ref[k, lo:lo + PB, :] = jnp.dot(pmt, yp[k], preferred_element_type=f32).astype(bf16)
    for k in heads:
        carry_ref[k] = carry[k]


def _lru(l, p, pm, pmt, conv_w, conv_b, w_a, w_x, b_a, b_x, lam, batch, nt):
    m = p.shape[1]
    width = HP_LRU * DB
    groups = width // LANES
    vec = pl.BlockSpec((None, 1, width), lambda b, h, t: (l, 0, h))
    mat = pl.BlockSpec((PB, PB), lambda b, h, t: (0, 0))
    gate_w = pl.BlockSpec((None, HP_LRU, DB, DB), lambda b, h, t: (l, h, 0, 0))
    return pl.pallas_call(
        _lru_kernel,
        grid=(batch, H_B // HP_LRU, nt),
        in_specs=[
            pl.BlockSpec((groups, T_LRU, LANES), lambda b, h, t: (G_XB // groups + h, b * nt + t, 0)),
            pl.BlockSpec((groups, T_LRU, LANES), lambda b, h, t: (G_ZB // groups + h, b * nt + t, 0)),
            mat, mat,
            pl.BlockSpec((None, CONV_W, width), lambda b, h, t: (l, 0, h)),
            vec, gate_w, gate_w, vec, vec, vec,
        ],
        out_specs=pl.BlockSpec((HP_LRU, T_LRU, DB), lambda b, h, t: (h, b * nt + t, 0)),
        out_shape=jax.ShapeDtypeStruct((H_B, m, DB), bf16),
        scratch_shapes=[pltpu.VMEM((HP_LRU, HALO, DB), f32), pltpu.VMEM((HP_LRU, SUBLANES, DB), f32)],
        compiler_params=_params(3),
        name="rglru",
    )(p, p, pm, pmt, conv_w, conv_b, w_a, w_x, b_a, b_x, lam)


def _outproj_kernel(ya_ref, yb_ref, w_ref, x_ref, o_ref):
    y = jnp.concatenate([ya_ref[h] for h in range(H_A)] + [yb_ref[h] for h in range(H_B)], axis=-1)
    o_ref[...] = x_ref[...] + jnp.dot(y, w_ref[...], preferred_element_type=f32)


def _outproj(l, ya, yb, w_out, x2d):
    m = x2d.shape[0]
    return pl.pallas_call(
        _outproj_kernel,
        grid=(D_MODEL // TN_OUT, m // TM_OUT),
        in_specs=[
            pl.BlockSpec((H_A, TM_OUT, DV_A), lambda j, i: (0, i, 0)),
            pl.BlockSpec((H_B, TM_OUT, DB), lambda j, i: (0, i, 0)),
            pl.BlockSpec((None, 2 * D_MODEL, TN_OUT), lambda j, i: (l, 0, j)),
            pl.BlockSpec((TM_OUT, TN_OUT), lambda j, i: (i, j)),
        ],
        out_specs=pl.BlockSpec((TM_OUT, TN_OUT), lambda j, i: (i, j)),
        out_shape=jax.ShapeDtypeStruct((m, D_MODEL), f32),
        compiler_params=_params(2),
        name="outproj",
    )(ya, yb, w_out, x2d)


def _norm_kernel(x_ref, g_ref, o_ref):
    x = x_ref[...]
    ms = jnp.mean(x * x, axis=-1, keepdims=True)
    o_ref[...] = (x * lax.rsqrt(ms + EPS)) * g_ref[...]


def _final_norm(x2d, g):
    m = x2d.shape[0]
    return pl.pallas_call(
        _norm_kernel,
        grid=(m // TM_NORM,),
        in_specs=[pl.BlockSpec((TM_NORM, D_MODEL), lambda i: (i, 0)),
                  pl.BlockSpec((1, D_MODEL), lambda i: (0, 0))],
        out_specs=pl.BlockSpec((TM_NORM, D_MODEL), lambda i: (i, 0)),
        out_shape=jax.ShapeDtypeStruct((m, D_MODEL), f32),
        compiler_params=_params(1),
        name="final_norm",
    )(x2d, g)


def _pad_lanes(v):
    return jnp.pad(v.astype(f32), ((0, 0), (0, LANES - v.shape[1])))[:, None, :]


def kernel(x, norm_g, w_in, i_bias, f_bias, qk_conv, head_norm_g, lru_conv_w, lru_conv_b,
           w_a, b_a, w_x, b_x, lam, w_out, final_g):
    batch, seq, d = x.shape
    m = batch * seq
    nc = seq // LC
    nt = seq // T_LRU
    w_main, w_gate = _prep_weights(jnp.swapaxes(w_in, 1, 2))
    w_out_b = w_out.astype(bf16)
    w_a_b = (0.5 * w_a).astype(bf16)
    w_x_b = (0.5 * w_x).astype(bf16)
    ib = _pad_lanes(i_bias)
    fb = _pad_lanes(f_bias)
    row = lambda v: v[:, None, :]
    pm = jnp.asarray(_PM, dtype=bf16)
    pmt = jnp.asarray(_PM.T, dtype=bf16)

    x2d = x.reshape(m, d)
    for l in range(DEPTH):
        p, a, wi, e, wk, rt, dec = _inproj(l, x2d, row(norm_g), w_main, w_gate, ib, fb, seq // TM_IN)
        ya = _mlstm(l, p, a, wi, e, wk, rt, dec, pm, pmt, qk_conv, row(head_norm_g), batch, nc)
        yb = _lru(l, p, pm, pmt, lru_conv_w, row(lru_conv_b), w_a_b, w_x_b, row(0.5 * b_a), row(0.5 * b_x), row(lam),
                  batch, nt)
        x2d = _outproj(l, ya, yb, w_out_b, x2d)
    return _final_norm(x2d, final_g[None, :]).reshape(batch, seq, d)
```

```python
import numpy as np

import jax
import jax.numpy as jnp
from jax import lax
from jax.experimental import pallas as pl
from jax.experimental.pallas import tpu as pltpu

f32 = jnp.float32
bf16 = jnp.bfloat16

D_MODEL = 2048
DEPTH = 4
H_A = 8
DK_A = 128
DV_A = 256
H_B = 8
DB = 256
CONV_W = 4
LRU_C = 8.0
EPS = 1e-6

LANES = 128
SUBLANES = 8
N_MAIN = 6 * D_MODEL
N_IN = N_MAIN + 2 * H_A
C_GATE = 4 * D_MODEL
C_O = 2 * D_MODEL
C_ZB_MAIN = 5 * D_MODEL
N_GROUPS = N_MAIN // LANES
G_Q, G_K, G_V, G_O, G_ZA, G_XB, G_ZB = 0, 8, 16, 32, 48, 64, 80
N_GROUPS_A = 64

N_GATE = 2 * H_A
TQ_PREP = 1024
Q_GATE = C_GATE // TQ_PREP
TM_IN = 1024
TN_IN = 1536
PB = 256
SEG = PB // SUBLANES
LC = PB
N_CHUNKS_IN = TM_IN // LC
T_LRU = 1024
HP_LRU = 4
TM_OUT = 512
TN_OUT = 1024
TM_NORM = 1024
HALO = (CONV_W - 1) * SUBLANES

NEG_BIG = -1e30
VMEM_LIMIT = 56 * 1024 * 1024


def _segment_major_matrix():
    p = np.arange(PB)
    pm = np.zeros((PB, PB), np.float32)
    pm[p, (p % SUBLANES) * SEG + p // SUBLANES] = 1.0
    return pm


_PM = _segment_major_matrix()


def _sigmoid_of_half(hx):
    return 0.5 * jnp.tanh(hx) + 0.5


def _silu_of_half(hx):
    return hx * jnp.tanh(hx) + hx


def _silu(x):
    return _silu_of_half(0.5 * x)


def _params(n_axes):
    return pltpu.CompilerParams(dimension_semantics=("arbitrary",) * n_axes, vmem_limit_bytes=VMEM_LIMIT)


def _prep_kernel(wt_ref, nxt_ref, wm_ref, wg_ref):
    q = pl.program_id(1)
    col0 = q * TQ_PREP
    halved = jnp.logical_or(jnp.logical_and(col0 >= C_O, col0 < C_GATE), col0 >= C_ZB_MAIN)
    scale = jnp.where(halved, 0.5, 1.0)

    @pl.when(q < Q_GATE)
    def _():
        wm_ref[...] = (wt_ref[...] * scale).T.astype(bf16)

    @pl.when(q >= Q_GATE)
    def _():
        rows = jnp.concatenate([wt_ref[N_GATE:TQ_PREP, :], nxt_ref[...]], axis=0)
        wm_ref[...] = (rows * scale).T.astype(bf16)

    @pl.when(q == Q_GATE)
    def _():
        zeros = jnp.zeros((LANES - H_A, D_MODEL), f32)
        wg_ref[...] = jnp.concatenate([wt_ref[0:H_A, :], zeros, wt_ref[H_A:N_GATE, :], zeros],
                                      axis=0).T.astype(bf16)


def _prep_weights(w_in_t):
    nxt_block = lambda l, q: (l, jnp.maximum(q + 1 - Q_GATE, 0) * (TQ_PREP // N_GATE) + C_GATE // N_GATE, 0)
    return pl.pallas_call(
        _prep_kernel,
        grid=(DEPTH, N_MAIN // TQ_PREP),
        in_specs=[pl.BlockSpec((None, TQ_PREP, D_MODEL), lambda l, q: (l, q, 0)),
                  pl.BlockSpec((None, N_GATE, D_MODEL), nxt_block)],
        out_specs=[pl.BlockSpec((None, D_MODEL, TQ_PREP), lambda l, q: (l, 0, q)),
                   pl.BlockSpec((None, D_MODEL, 2 * LANES), lambda l, q: (l, 0, 0))],
        out_shape=[jax.ShapeDtypeStruct((DEPTH, D_MODEL, N_MAIN), bf16),
                   jax.ShapeDtypeStruct((DEPTH, D_MODEL, 2 * LANES), bf16)],
        compiler_params=_params(2),
        name="prep_weights",
    )(w_in_t, w_in_t)


def _gate_chunk(gi, gf, ib, fb, m_st):
    ii = gi + ib
    lf = jax.nn.log_sigmoid(gf + fb)
    row = lax.broadcasted_iota(jnp.int32, (LC, LANES), 0)
    bc = lf
    d = 1
    while d < LC:
        bc = bc + jnp.where(row >= d, pltpu.roll(bc, d, axis=0), 0.0)
        d *= 2
    r = ii - bc
    cm = r
    d = 1
    while d < LC:
        cm = jnp.maximum(cm, jnp.where(row >= d, pltpu.roll(cm, d, axis=0), NEG_BIG))
        d *= 2
    mx = jnp.maximum(m_st, cm)
    mx_last = mx[LC - 1:LC, :]
    return (-mx, jnp.exp(m_st - mx), jnp.exp(-(bc + mx)), jnp.exp(r - mx_last),
            r.T[0:SUBLANES, :], jnp.exp(m_st - mx_last), bc[LC - 1:LC, :] + mx_last)


def _inproj_kernel(tiles_per_seq, x_ref, g_ref, w_ref, wg_ref, ib_ref, fb_ref,
                   p_ref, a_ref, wi_ref, e_ref, wk_ref, rt_ref, dec_ref, u_ref, gg_ref, m_ref):
    i = pl.program_id(0)
    j = pl.program_id(1)

    def matmul():
        acc = jnp.dot(u_ref[...], w_ref[...], preferred_element_type=f32)
        for g in range(TN_IN // LANES):
            p_ref[g] = acc[:, g * LANES:(g + 1) * LANES].astype(bf16)

    @pl.when(j == 0)
    def _():
        @pl.when(i % tiles_per_seq == 0)
        def _():
            m_ref[...] = jnp.zeros_like(m_ref)

        x = x_ref[...]
        ms = jnp.mean(x * x, axis=-1, keepdims=True)
        ub = ((x * lax.rsqrt(ms + EPS)) * g_ref[...]).astype(bf16)
        u_ref[...] = ub
        gg_ref[...] = jnp.dot(ub, wg_ref[...], preferred_element_type=f32)
        matmul()

    @pl.when(jnp.logical_and(j >= 1, j <= N_CHUNKS_IN))
    def _():
        c = j - 1
        rows = pl.ds(pl.multiple_of(c * LC, LC), LC)
        a, wi, e, wk, rt, dec, m_new = _gate_chunk(gg_ref[rows, :LANES], gg_ref[rows, LANES:],
                                                   ib_ref[...], fb_ref[...], m_ref[0:1, :])
        a_ref[rows, :] = a
        wi_ref[rows, :] = wi
        e_ref[rows, :] = e
        wk_ref[rows, :] = wk
        rt_ref[c] = rt
        dec_ref[c] = jnp.broadcast_to(dec, (SUBLANES, LANES))
        m_ref[...] = jnp.broadcast_to(m_new, (SUBLANES, LANES))
        matmul()

    @pl.when(j > N_CHUNKS_IN)
    def _():
        matmul()


def _inproj(l, x2d, norm_g, w_main, w_gate, ib, fb, tiles_per_seq):
    m = x2d.shape[0]
    grid = (m // TM_IN, N_MAIN // TN_IN)
    col = pl.BlockSpec((TM_IN, LANES), lambda i, j: (i, 0))
    vec = pl.BlockSpec((None, 1, LANES), lambda i, j: (l, 0, 0))
    kern = lambda *refs: _inproj_kernel(tiles_per_seq, *refs)
    return pl.pallas_call(
        kern,
        grid=grid,
        in_specs=[
            pl.BlockSpec((TM_IN, D_MODEL), lambda i, j: (i, 0)),
            pl.BlockSpec((None, 1, D_MODEL), lambda i, j: (l, 0, 0)),
            pl.BlockSpec((None, D_MODEL, TN_IN), lambda i, j: (l, 0, j)),
            pl.BlockSpec((None, D_MODEL, 2 * LANES), lambda i, j: (l, 0, 0)),
            vec, vec,
        ],
        out_specs=[
            pl.BlockSpec((TN_IN // LANES, TM_IN, LANES), lambda i, j: (j, i, 0)),
            col, col, col, col,
            pl.BlockSpec((N_CHUNKS_IN, SUBLANES, LC), lambda i, j: (i, 0, 0)),
            pl.BlockSpec((N_CHUNKS_IN, SUBLANES, LANES), lambda i, j: (i, 0, 0)),
        ],
        out_shape=[jax.ShapeDtypeStruct((N_GROUPS, m, LANES), bf16)]
        + [jax.ShapeDtypeStruct((m, LANES), f32)] * 4
        + [jax.ShapeDtypeStruct((m // LC, SUBLANES, LC), f32),
           jax.ShapeDtypeStruct((m // LC, SUBLANES, LANES), f32)],
        scratch_shapes=[pltpu.VMEM((TM_IN, D_MODEL), bf16),
                        pltpu.VMEM((TM_IN, 2 * LANES), f32),
                        pltpu.VMEM((SUBLANES, LANES), f32)],
        compiler_params=_params(2),
        name="inproj",
    )(x2d, norm_g, w_main, w_gate, ib, fb)


def _causal_conv_block(cur, prev_tail, w, row8):
    tail = cur[PB - HALO:PB]
    before = [pltpu.roll(jnp.where(row8 == SUBLANES - 1,
                                   prev_tail[i * SUBLANES:(i + 1) * SUBLANES],
                                   tail[i * SUBLANES:(i + 1) * SUBLANES]), 1, axis=0)
              for i in range(CONV_W - 1)]
    ext = jnp.concatenate(before + [cur], axis=0)
    acc = cur * w[CONV_W - 1:CONV_W, :]
    for j in range(1, CONV_W):
        lo = HALO - j * SUBLANES
        acc = acc + ext[lo:lo + PB] * w[CONV_W - 1 - j:CONV_W - j, :]
    return acc, tail


def _mlstm_kernel(p_ref, a_ref, wi_ref, e_ref, wk_ref, rt_ref, dec_ref, pm_ref, pmt_ref, cw_ref, hng_ref,
                  ya_ref, c_ref, n_ref, halo_ref):
    @pl.when(pl.program_id(1) == 0)
    def _():
        c_ref[...] = jnp.zeros_like(c_ref)
        n_ref[...] = jnp.zeros_like(n_ref)
        halo_ref[...] = jnp.zeros_like(halo_ref)

    row8 = lax.broadcasted_iota(jnp.int32, (SUBLANES, 2 * DK_A), 0)
    causal = (lax.broadcasted_iota(jnp.int32, (LC, LC), 0)
              >= lax.broadcasted_iota(jnp.int32, (LC, LC), 1))
    lane = lax.broadcasted_iota(jnp.int32, (1, 2 * DK_A), 1)
    qk_scale = jnp.where(lane < DK_A, 1.0, DK_A ** -0.5)
    pm = pm_ref[...]
    pmt = pmt_ref[...]

    def wide(g0):
        return jnp.concatenate([p_ref[g0], p_ref[g0 + 1]], axis=-1)

    heads = range(H_A)
    qk_seg = [jnp.dot(pm, jnp.concatenate([p_ref[G_Q + h], p_ref[G_K + h]], axis=-1),
                      preferred_element_type=f32) for h in heads]
    acts = []
    for h in heads:
        w_qk = jnp.concatenate([cw_ref[:, (G_Q + h) * LANES:(G_Q + h + 1) * LANES],
                                cw_ref[:, (G_K + h) * LANES:(G_K + h + 1) * LANES]], axis=-1)
        y, tail = _causal_conv_block(qk_seg[h], halo_ref[h], w_qk, row8)
        halo_ref[h] = tail
        acts.append((_silu(y) * qk_scale).astype(bf16))
    qk = [jnp.dot(pmt, acts[h], preferred_element_type=f32) for h in heads]
    qb = [qk[h][:, :DK_A].astype(bf16) for h in heads]
    kb = [qk[h][:, DK_A:].astype(bf16) for h in heads]
    v = [wide(G_V + 2 * h) for h in heads]

    s_raw = [lax.dot_general(qb[h], kb[h], (((1,), (1,)), ((), ())), preferred_element_type=f32)
             for h in heads]
    q_c = [jnp.dot(qb[h], c_ref[h].astype(bf16), preferred_element_type=f32) for h in heads]
    s = []
    half = LC // 2
    for h in heads:
        a_col = a_ref[:, h:h + 1]
        r_row = rt_ref[h:h + 1, :]
        top = s_raw[h][:half, :half] * jnp.exp(
            jnp.where(causal[:half, :half], a_col[:half] + r_row[:, :half], NEG_BIG))
        bottom = s_raw[h][half:] * jnp.exp(jnp.where(causal[half:], a_col[half:] + r_row, NEG_BIG))
        s.append(jnp.concatenate(
            [jnp.concatenate([top, jnp.zeros((half, half), f32)], axis=1), bottom], axis=0))
    kw = [qk[h][:, DK_A:] * wk_ref[:, h:h + 1] for h in heads]
    both = [jnp.dot(jnp.concatenate([s[h].astype(bf16), kw[h].T.astype(bf16)], axis=0), v[h],
                    preferred_element_type=f32) for h in heads]
    s_v = [both[h][:LC] for h in heads]
    kw_v = [both[h][LC:] for h in heads]
    hc = []
    for h in heads:
        wi = wi_ref[:, h:h + 1]
        n_st = n_ref[h]
        dec = dec_ref[0:1, h:h + 1]
        den = (jnp.sum(s[h], axis=-1, keepdims=True)
               + wi * jnp.sum(qk[h][:, :DK_A] * n_st, axis=-1, keepdims=True))
        hc.append((s_v[h] + wi * q_c[h]) * (1.0 / jnp.maximum(jnp.abs(den), e_ref[:, h:h + 1])))
        c_ref[h] = dec * c_ref[h] + kw_v[h]
        n_ref[h] = dec * n_st + jnp.sum(kw[h], axis=0, keepdims=True)

    for h in heads:
        hg = hc[h] * _sigmoid_of_half(wide(G_O + 2 * h).astype(f32))
        mu = jnp.mean(hg, axis=-1, keepdims=True)
        dev = hg - mu
        var = jnp.mean(dev * dev, axis=-1, keepdims=True)
        hn = (dev * lax.rsqrt(var + EPS)) * hng_ref[:, h * DV_A:(h + 1) * DV_A]
        ya_ref[h] = (hn * _silu_of_half(wide(G_ZA + 2 * h).astype(f32))).astype(bf16)


def _mlstm(l, p, a, wi, e, wk, rt, dec, pm, pmt, conv_w, hn_g, batch, nc):
    m = p.shape[1]
    blk = pl.BlockSpec((LC, LANES), lambda b, c: (b * nc + c, 0))
    mat = pl.BlockSpec((PB, PB), lambda b, c: (0, 0))
    return pl.pallas_call(
        _mlstm_kernel,
        grid=(batch, nc),
        in_specs=[
            pl.BlockSpec((N_GROUPS_A, LC, LANES), lambda b, c: (0, b * nc + c, 0)),
            blk, blk, blk, blk,
            pl.BlockSpec((None, SUBLANES, LC), lambda b, c: (b * nc + c, 0, 0)),
            pl.BlockSpec((None, SUBLANES, LANES), lambda b, c: (b * nc + c, 0, 0)),
            mat, mat,
            pl.BlockSpec((None, CONV_W, 2 * H_A * DK_A), lambda b, c: (l, 0, 0)),
            pl.BlockSpec((None, 1, H_A * DV_A), lambda b, c: (l, 0, 0)),
        ],
        out_specs=pl.BlockSpec((H_A, LC, DV_A), lambda b, c: (0, b * nc + c, 0)),
        out_shape=jax.ShapeDtypeStruct((H_A, m, DV_A), bf16),
        scratch_shapes=[
            pltpu.VMEM((H_A, DK_A, DV_A), f32),
            pltpu.VMEM((H_A, 1, DK_A), f32),
            pltpu.VMEM((H_A, HALO, 2 * DK_A), f32),
        ],
        compiler_params=_params(2),
        name="mlstm",
    )(p, a, wi, e, wk, rt, dec, pm, pmt, conv_w, hn_g)


def _lru_kernel(xb_ref, zb_ref, pm_ref, pmt_ref, cw_ref, cb_ref, wa_ref, wx_ref, ba_ref, bx_ref, lam_ref,
                yb_ref, halo_ref, carry_ref):
    @pl.when(pl.program_id(2) == 0)
    def _():
        halo_ref[...] = jnp.zeros_like(halo_ref)
        carry_ref[...] = jnp.zeros_like(carry_ref)

    n_blocks = T_LRU // PB
    heads = range(HP_LRU)
    row8 = lax.broadcasted_iota(jnp.int32, (SUBLANES, DB), 0)
    pm = pm_ref[...]
    pmt = pmt_ref[...]

    def wide(ref, k):
        return jnp.concatenate([ref[2 * k], ref[2 * k + 1]], axis=-1)

    def chan(ref, k):
        return ref[:, k * DB:(k + 1) * DB]

    x_seg = [[jnp.dot(pm, wide(xb_ref, k)[b * PB:(b + 1) * PB], preferred_element_type=f32)
              for b in range(n_blocks)] for k in heads]
    z_seg = [[jnp.dot(pm, wide(zb_ref, k)[b * PB:(b + 1) * PB], preferred_element_type=f32)
              for b in range(n_blocks)] for k in heads]

    xc = []
    for k in heads:
        prev_tail = halo_ref[k]
        blocks = []
        for b in range(n_blocks):
            y, prev_tail = _causal_conv_block(x_seg[k][b], prev_tail, chan(cw_ref, k), row8)
            blocks.append(y + chan(cb_ref, k))
        halo_ref[k] = prev_tail
        xc.append(jnp.concatenate(blocks, axis=0))

    xcb = [xc[k].astype(bf16) for k in heads]
    pre_r = [jnp.dot(xcb[k], wa_ref[k], preferred_element_type=f32) + chan(ba_ref, k) for k in heads]
    pre_i = [jnp.dot(xcb[k], wx_ref[k], preferred_element_type=f32) + chan(bx_ref, k) for k in heads]
    a, u = [], []
    for k in heads:
        lam = chan(lam_ref, k)
        softplus_neg = jnp.maximum(-lam, 0.0) + jnp.log1p(jnp.exp(-jnp.abs(lam)))
        half_rate = (-0.5 * LRU_C) * softplus_neg
        log_a = half_rate * jnp.tanh(pre_r[k]) + half_rate
        ak = jnp.exp(log_a)
        a.append(ak)
        var_in = jnp.tanh(-log_a) * (ak * ak + 1.0)
        scale_in = jnp.where(var_in > 0.0, var_in * lax.rsqrt(var_in), 0.0)
        u.append((xc[k] * _sigmoid_of_half(pre_i[k])) * scale_in)

    carry = [carry_ref[k] for k in heads]
    for b in range(n_blocks):
        lo = b * PB
        yp = []
        for k in heads:
            gate = _silu_of_half(z_seg[k][b])
            h = u[k][lo:lo + SUBLANES]
            p = a[k][lo:lo + SUBLANES]
            hs, ps = [h], [p]
            for j in range(1, SEG):
                aj = a[k][lo + j * SUBLANES:lo + (j + 1) * SUBLANES]
                h = aj * h + u[k][lo + j * SUBLANES:lo + (j + 1) * SUBLANES]
                p = aj * p
                hs.append(h)
                ps.append(p)
            for d in (1, 2, 4):
                keep = row8 >= d
                h_sh = jnp.where(keep, pltpu.roll(h, d, axis=0), 0.0)
                p_sh = jnp.where(keep, pltpu.roll(p, d, axis=0), 1.0)
                h = h + p * h_sh
                p = p * p_sh
            seg_end = h + p * carry[k]
            seg_in = jnp.where(row8 == 0, carry[k], pltpu.roll(seg_end, 1, axis=0))
            carry[k] = jnp.broadcast_to(seg_end[SUBLANES - 1:SUBLANES, :], (SUBLANES, DB))
            yp.append(jnp.concatenate(
                [(hs[j] + ps[j] * seg_in) * gate[j * SUBLANES:(j + 1) * SUBLANES] for j in range(SEG)],
                axis=0).astype(bf16))
        for k in heads:
            yb_ref[k, lo:lo + PB, :] = jnp.dot(pmt, yp[k], preferred_element_type=f32).astype(bf16)
    for k in heads:
        carry_ref[k] = carry[k]


def _lru(l, p, pm, pmt, conv_w, conv_b, w_a, w_x, b_a, b_x, lam, batch, nt):
    m = p.shape[1]
    width = HP_LRU * DB
    groups = width // LANES
    vec = pl.BlockSpec((None, 1, width), lambda b, h, t: (l, 0, h))
    mat = pl.BlockSpec((PB, PB), lambda b, h, t: (0, 0))
    gate_w = pl.BlockSpec((None, HP_LRU, DB, DB), lambda b, h, t: (l, h, 0, 0))
    return pl.pallas_call(
        _lru_kernel,
        grid=(batch, H_B // HP_LRU, nt),
        in_specs=[
            pl.BlockSpec((groups, T_LRU, LANES), lambda b, h, t: (G_XB // groups + h, b * nt + t, 0)),
            pl.BlockSpec((groups, T_LRU, LANES), lambda b, h, t: (G_ZB // groups + h, b * nt + t, 0)),
            mat, mat,
            pl.BlockSpec((None, CONV_W, width), lambda b, h, t: (l, 0, h)),
            vec, gate_w, gate_w, vec, vec, vec,
        ],
        out_specs=pl.BlockSpec((HP_LRU, T_LRU, DB), lambda b, h, t: (h, b * nt + t, 0)),
        out_shape=jax.ShapeDtypeStruct((H_B, m, DB), bf16),
        scratch_shapes=[pltpu.VMEM((HP_LRU, HALO, DB), f32), pltpu.VMEM((HP_LRU, SUBLANES, DB), f32)],
        compiler_params=_params(3),
        name="rglru",
    )(p, p, pm, pmt, conv_w, conv_b, w_a, w_x, b_a, b_x, lam)


def _outproj_kernel(ya_ref, yb_ref, w_ref, x_ref, o_ref):
    y = jnp.concatenate([ya_ref[h] for h in range(H_A)] + [yb_ref[h] for h in range(H_B)], axis=-1)
    o_ref[...] = x_ref[...] + jnp.dot(y, w_ref[...], preferred_element_type=f32)


def _outproj(l, ya, yb, w_out, x2d):
    m = x2d.shape[0]
    return pl.pallas_call(
        _outproj_kernel,
        grid=(D_MODEL // TN_OUT, m // TM_OUT),
        in_specs=[
            pl.BlockSpec((H_A, TM_OUT, DV_A), lambda j, i: (0, i, 0)),
            pl.BlockSpec((H_B, TM_OUT, DB), lambda j, i: (0, i, 0)),
            pl.BlockSpec((None, 2 * D_MODEL, TN_OUT), lambda j, i: (l, 0, j)),
            pl.BlockSpec((TM_OUT, TN_OUT), lambda j, i: (i, j)),
        ],
        out_specs=pl.BlockSpec((TM_OUT, TN_OUT), lambda j, i: (i, j)),
        out_shape=jax.ShapeDtypeStruct((m, D_MODEL), f32),
        compiler_params=_params(2),
        name="outproj",
    )(ya, yb, w_out, x2d)


def _norm_kernel(x_ref, g_ref, o_ref):
    x = x_ref[...]
    ms = jnp.mean(x * x, axis=-1, keepdims=True)
    o_ref[...] = (x * lax.rsqrt(ms + EPS)) * g_ref[...]


def _final_norm(x2d, g):
    m = x2d.shape[0]
    return pl.pallas_call(
        _norm_kernel,
        grid=(m // TM_NORM,),
        in_specs=[pl.BlockSpec((TM_NORM, D_MODEL), lambda i: (i, 0)),
                  pl.BlockSpec((1, D_MODEL), lambda i: (0, 0))],
        out_specs=pl.BlockSpec((TM_NORM, D_MODEL), lambda i: (i, 0)),
        out_shape=jax.ShapeDtypeStruct((m, D_MODEL), f32),
        compiler_params=_params(1),
        name="final_norm",
    )(x2d, g)


def _pad_lanes(v):
    return jnp.pad(v.astype(f32), ((0, 0), (0, LANES - v.shape[1])))[:, None, :]


def kernel(x, norm_g, w_in, i_bias, f_bias, qk_conv, head_norm_g, lru_conv_w, lru_conv_b,
           w_a, b_a, w_x, b_x, lam, w_out, final_g):
    batch, seq, d = x.shape
    m = batch * seq
    nc = seq // LC
    nt = seq // T_LRU
    w_main, w_gate = _prep_weights(jnp.swapaxes(w_in, 1, 2))
    w_out_b = w_out.astype(bf16)
    w_a_b = (0.5 * w_a).astype(bf16)
    w_x_b = (0.5 * w_x).astype(bf16)
    ib = _pad_lanes(i_bias)
    fb = _pad_lanes(f_bias)
    row = lambda v: v[:, None, :]
    pm = jnp.asarray(_PM, dtype=bf16)
    pmt = jnp.asarray(_PM.T, dtype=bf16)

    x2d = x.reshape(m, d)
    for l in range(DEPTH):
        p, a, wi, e, wk, rt, dec = _inproj(l, x2d, row(norm_g), w_main, w_gate, ib, fb, seq // TM_IN)
        ya = _mlstm(l, p, a, wi, e, wk, rt, dec, pm, pmt, qk_conv, row(head_norm_g), batch, nc)
        yb = _lru(l, p, pm, pmt, lru_conv_w, row(lru_conv_b), w_a_b, w_x_b, row(0.5 * b_a), row(0.5 * b_x), row(lam),
                  batch, nt)
        x2d = _outproj(l, ya, yb, w_out_b, x2d)
    return _final_norm(x2d, final_g[None, :]).reshape(batch, seq, d)
```
